```python
import math
import jax, jax.numpy as jnp
from jax import lax
import numpy as np


D_MODEL = 1024
BATCH = 16
SEQ = 2048
DEPTH = 4
DEC_BATCH = 16
DEC_SEQ = 16
PAST_LEN = 4096

CHUNK = 64
QBLK = 128
HEAD_DIM = 64
DA_HEADS = 4
DA_WIDTH = DA_HEADS * 2 * HEAD_DIM
SB_HEADS = 8
SB_WIDTH = SB_HEADS * HEAD_DIM
ROPE_THETA = 10000.0
EPS = 1e-6
NEG_INF = -1e30
IN_COLS = 4 * DA_WIDTH + 4 * SB_WIDTH + 2 * D_MODEL
IN_SPLITS = [int(i) for i in np.cumsum([DA_WIDTH] * 4 + [SB_WIDTH] * 4 + [D_MODEL])]

kernel_name = 'hybrid_diffattn_stickbreak_stream_step'


def rms_norm(x, w):
    xf = x.astype(jnp.float32)
    y = xf * lax.rsqrt(jnp.mean(xf * xf, axis=-1, keepdims=True) + EPS)
    return (y * w.astype(jnp.float32)).astype(x.dtype)


def rope(x, pos):
    half = HEAD_DIM // 2
    inv = ROPE_THETA ** (-jnp.arange(half, dtype=jnp.float32) / half)
    ang = pos.astype(jnp.float32)[:, None] * inv[None, :]
    bshape = (1, pos.shape[0]) + (1,) * (x.ndim - 3) + (half,)
    cos = jnp.cos(ang).reshape(bshape)
    sin = jnp.sin(ang).reshape(bshape)
    xf = x.astype(jnp.float32)
    x1, x2 = xf[..., :half], xf[..., half:]
    return jnp.concatenate([x1 * cos - x2 * sin, x2 * cos + x1 * sin], axis=-1).astype(x.dtype)


def diff_attn_block(q, k, v, q_pos, k_pos, lam):
    s = jnp.einsum('bqhmd,bkhmd->mbhqk', q, k).astype(jnp.float32) * (HEAD_DIM ** -0.5)
    mask = (k_pos[None, :] // CHUNK) <= (q_pos[:, None] // CHUNK)
    p = jax.nn.softmax(jnp.where(mask, s, NEG_INF), axis=-1)
    w = p[0] - lam * p[1]
    return jnp.einsum('bhqk,bkhe->bqhe', w.astype(v.dtype), v)


def sb_attn_block(q, k, v, q_pos, k_pos):
    z = jnp.einsum('bqhd,bkhd->bhqk', q, k).astype(jnp.float32) * (HEAD_DIM ** -0.5)
    mask = k_pos[None, :] < q_pos[:, None]
    log_fail = jnp.where(mask, jax.nn.log_sigmoid(-z), 0.0)
    after = lax.cumsum(log_fail, axis=3, reverse=True) - log_fail
    a = jnp.where(mask, jnp.exp(jax.nn.log_sigmoid(z) + after), 0.0)
    return jnp.einsum('bhqk,bkhd->bqhd', a.astype(v.dtype), v)


def sweep_self(fn, q, k, v, pos, *extra):
    t = q.shape[1]
    if t <= QBLK:
        return fn(q, k, v, pos, pos, *extra)
    outs = []
    for blk in range(t // QBLK):
        lo, hi = blk * QBLK, (blk + 1) * QBLK
        outs.append(fn(q[:, lo:hi], k[:, :hi], v[:, :hi], pos[lo:hi], pos[:hi], *extra))
    return jnp.concatenate(outs, axis=1)


def mixer_layer(x, q_pos, past, norm_w, w_in, lam_qk, subln_w, w_proj_da, w_proj_sb, w_out, lam_init):
    b, t, _ = x.shape
    h = rms_norm(x, norm_w)
    u = h @ w_in
    qd, kd, vd, gd, qs, ks, vs, gs, ga, gb = jnp.split(u, IN_SPLITS, axis=-1)
    qd = rope(qd.reshape(b, t, DA_HEADS, 2, HEAD_DIM), q_pos)
    kd = rope(kd.reshape(b, t, DA_HEADS, 2, HEAD_DIM), q_pos).reshape(b, t, DA_HEADS, 2 * HEAD_DIM)
    vd = vd.reshape(b, t, DA_HEADS, 2 * HEAD_DIM)
    qs = qs.reshape(b, t, SB_HEADS, HEAD_DIM)
    ks = ks.reshape(b, t, SB_HEADS, HEAD_DIM)
    vs = vs.reshape(b, t, SB_HEADS, HEAD_DIM)
    lf = lam_qk.astype(jnp.float32)
    lam = jnp.exp(jnp.sum(lf[0] * lf[1])) - jnp.exp(jnp.sum(lf[2] * lf[3])) + lam_init
    if past is None:
        o_da = sweep_self(diff_attn_block, qd, kd.reshape(b, t, DA_HEADS, 2, HEAD_DIM), vd, q_pos, lam)
        o_sb = sweep_self(sb_attn_block, qs, ks, vs, q_pos)
    else:
        pk_da, pv_da, pk_sb, pv_sb = past
        n_past = pk_da.shape[1]
        k_pos = jnp.arange(n_past + t, dtype=jnp.int32)
        kd_all = jnp.concatenate([pk_da, kd], axis=1).reshape(b, n_past + t, DA_HEADS, 2, HEAD_DIM)
        vd_all = jnp.concatenate([pv_da, vd], axis=1)
        ks_all = jnp.concatenate([pk_sb, ks], axis=1)
        vs_all = jnp.concatenate([pv_sb, vs], axis=1)
        o_da = diff_attn_block(qd, kd_all, vd_all, q_pos, k_pos, lam)
        o_sb = sb_attn_block(qs, ks_all, vs_all, q_pos, k_pos)
    y_da = (rms_norm(o_da, subln_w) * (1.0 - lam_init)).reshape(b, t, DA_WIDTH) * jax.nn.silu(gd)
    y_sb = o_sb.reshape(b, t, SB_WIDTH) * jax.nn.silu(gs)
    merged = jax.nn.sigmoid(ga) * (y_da @ w_proj_da) + jax.nn.sigmoid(gb) * (y_sb @ w_proj_sb)
    return x + merged @ w_out, (kd, vd, ks, vs)


def setup_inputs(seed: int = 0) -> dict:
    key = jax.random.key(seed)
    ks = jax.random.split(key, 16)
    f32 = jnp.float32
    nrm = lambda k, shape, s: jax.random.normal(k, shape, f32) * s
    return {
        'x_prompt': nrm(ks[0], (BATCH, SEQ, D_MODEL), 1.0),
        'x_sample': nrm(ks[1], (DEC_BATCH, DEC_SEQ, D_MODEL), 1.0),
        'cache_da_k': nrm(ks[2], (DEPTH, DEC_BATCH, PAST_LEN, DA_HEADS, 2 * HEAD_DIM), 1.0),
        'cache_da_v': nrm(ks[3], (DEPTH, DEC_BATCH, PAST_LEN, DA_HEADS, 2 * HEAD_DIM), 1.0),
        'cache_sb_k': nrm(ks[4], (DEPTH, DEC_BATCH, PAST_LEN, SB_HEADS, HEAD_DIM), 1.0),
        'cache_sb_v': nrm(ks[5], (DEPTH, DEC_BATCH, PAST_LEN, SB_HEADS, HEAD_DIM), 1.0),
        'norm_w': 1.0 + nrm(ks[6], (DEPTH, D_MODEL), 0.02),
        'w_in': nrm(ks[7], (DEPTH, D_MODEL, IN_COLS), D_MODEL ** -0.5),
        'lambda_qk': nrm(ks[8], (DEPTH, 4, HEAD_DIM), 0.1),
        'subln_w': 1.0 + nrm(ks[9], (DEPTH, 2 * HEAD_DIM), 0.02),
        'w_proj_da': nrm(ks[10], (DEPTH, DA_WIDTH, D_MODEL), DA_WIDTH ** -0.5),
        'w_proj_sb': nrm(ks[11], (DEPTH, SB_WIDTH, D_MODEL), SB_WIDTH ** -0.5),
        'w_out': nrm(ks[12], (DEPTH, D_MODEL, D_MODEL), D_MODEL ** -0.5),
        'final_norm_w': 1.0 + nrm(ks[13], (D_MODEL,), 0.02),
    }


def reference(x_prompt, x_sample, cache_da_k, cache_da_v, cache_sb_k, cache_sb_v,
              norm_w, w_in, lambda_qk, subln_w, w_proj_da, w_proj_sb, w_out, final_norm_w):
    pos_p = jnp.arange(x_prompt.shape[1], dtype=jnp.int32)
    n_past = cache_da_k.shape[2]
    pos_s = n_past + jnp.arange(x_sample.shape[1], dtype=jnp.int32)
    xp, xs = x_prompt, x_sample
    rows_p, rows_s = [], []
    for l in range(DEPTH):
        lam_init = 0.8 - 0.6 * math.exp(-0.3 * l)
        args = (norm_w[l], w_in[l], lambda_qk[l], subln_w[l], w_proj_da[l], w_proj_sb[l], w_out[l], lam_init)
        xp, rp = mixer_layer(xp, pos_p, None, *args)
        xs, rs = mixer_layer(xs, pos_s, (cache_da_k[l], cache_da_v[l], cache_sb_k[l], cache_sb_v[l]), *args)
        rows_p.append(rp)
        rows_s.append(rs)
    y_prompt = rms_norm(xp, final_norm_w)
    y_sample = rms_norm(xs, final_norm_w)
    new_da_k_p = jnp.stack([r[0] for r in rows_p], axis=0)
    new_da_v_p = jnp.stack([r[1] for r in rows_p], axis=0)
    new_sb_k_p = jnp.stack([r[2] for r in rows_p], axis=0)
    new_sb_v_p = jnp.stack([r[3] for r in rows_p], axis=0)
    new_da_k_s = jnp.stack([r[0] for r in rows_s], axis=0)
    new_da_v_s = jnp.stack([r[1] for r in rows_s], axis=0)
    new_sb_k_s = jnp.stack([r[2] for r in rows_s], axis=0)
    new_sb_v_s = jnp.stack([r[3] for r in rows_s], axis=0)
    return (y_prompt, y_sample, new_da_k_p, new_da_v_p, new_sb_k_p, new_sb_v_p,
            new_da_k_s, new_da_v_s, new_sb_k_s, new_sb_v_s)
```

```python
import functools
import math

import jax
import jax.numpy as jnp
from jax import lax
from jax.experimental import pallas as pl
from jax.experimental.pallas import tpu as pltpu

F32 = jnp.float32
BF16 = jnp.bfloat16

CHUNK = 64
HEAD_DIM = 64
PAIR = 2 * HEAD_DIM
N_PAIRS = 4
WIDTH = N_PAIRS * PAIR
ROPE_THETA = 10000.0
EPS = 1e-6
NEG_INF = -1e30
QK_SCALE = HEAD_DIM ** -0.5

TOKEN_TILE = 256
ATTN_TILE = 256
CACHE_TILE = 1024
VMEM_LIMIT = 56 * 1024 * 1024

G_QD, G_KD, G_VD, G_GD, G_QS, G_KS, G_VS, G_GS, G_GA, G_GB = 0, 1, 2, 3, 4, 5, 6, 7, 8, 10


def _sigmoid(x):
    return 1.0 / (1.0 + jnp.exp(-x))


def _nt_dot(a, b):
    return lax.dot_general(a, b, (((1,), (1,)), ((), ())), preferred_element_type=F32)


def _dot(a, b):
    return jnp.dot(a, b, preferred_element_type=F32)


def _inproj_kernel(*refs, feature_major_sb):
    if feature_major_sb:
        x_ref, nw_ref, w_ref, cos_ref, sin_ref, wkv_ref = refs[:6]
        outs = refs[6:]
    else:
        x_ref, nw_ref, w_ref, cos_ref, sin_ref = refs[:5]
        outs = refs[5:]
    (qd_ref, kd_ref, vd_ref, gd_ref, qs_ref, ks_ref, vs_ref, gs_ref, ga_ref, gb_ref,
     kdb_ref, vdb_ref, ksb_ref, vsb_ref) = outs

    x = x_ref[...]
    ms = jnp.mean(x * x, axis=-1, keepdims=True)
    h = (x * lax.rsqrt(ms + EPS) * nw_ref[...]).astype(BF16)

    def proj(group):
        return _dot(h, w_ref[:, group * WIDTH:(group + 1) * WIDTH])

    lane = lax.broadcasted_iota(jnp.int32, (1, PAIR), 1)
    upper_half = (lane & (HEAD_DIM // 2)) != 0
    cos = cos_ref[...]
    sin = sin_ref[...]

    def rope(s):
        partner = jnp.where(upper_half, pltpu.roll(s, HEAD_DIM // 2, 1),
                            pltpu.roll(s, PAIR - HEAD_DIM // 2, 1))
        return s * cos + partner * sin

    def slab(u, c):
        return u[:, c * PAIR:(c + 1) * PAIR]

    u = proj(G_QD)
    for c in range(N_PAIRS):
        qd_ref[:, c * PAIR:(c + 1) * PAIR] = (rope(slab(u, c)) * QK_SCALE).astype(BF16)
    u = proj(G_KD)
    for c in range(N_PAIRS):
        kd = rope(slab(u, c))
        kd_ref[:, c, :] = kd
        kdb_ref[:, c * PAIR:(c + 1) * PAIR] = kd.astype(BF16)
    u = proj(G_VD)
    for c in range(N_PAIRS):
        vd_ref[:, c, :] = slab(u, c)
    vdb_ref[...] = u.astype(BF16)
    g = proj(G_GD)
    gd_ref[...] = (g * _sigmoid(g)).astype(BF16)
    qs_ref[...] = (proj(G_QS) * QK_SCALE).astype(BF16)
    if feature_major_sb:
        for half, (o_ref, ob_ref) in enumerate(((ks_ref, ksb_ref), (vs_ref, vsb_ref))):
            ut = _nt_dot(wkv_ref[half * WIDTH:(half + 1) * WIDTH, :], h)
            o_ref[...] = ut
            ob_ref[...] = ut.astype(BF16)
    else:
        for group, o_ref, ob_ref in ((G_KS, ks_ref, ksb_ref), (G_VS, vs_ref, vsb_ref)):
            u = proj(group)
            o_ref[...] = u
            ob_ref[...] = u.astype(BF16)
    g = proj(G_GS)
    gs_ref[...] = (g * _sigmoid(g)).astype(BF16)
    for half in range(2):
        ga_ref[:, half * WIDTH:(half + 1) * WIDTH] = _sigmoid(proj(G_GA + half)).astype(BF16)
        gb_ref[:, half * WIDTH:(half + 1) * WIDTH] = _sigmoid(proj(G_GB + half)).astype(BF16)


def _inproj(x, norm_w, w_in_bf, layer, cos_tab, sin_tab, wkv_t_bf=None):
    n, d = x.shape
    tm = TOKEN_TILE
    seq = cos_tab.shape[0]
    n_tab = seq // tm
    cols = w_in_bf.shape[-1]
    row = lambda i: (i, 0)
    wide = pl.BlockSpec((tm, WIDTH), row)
    full = pl.BlockSpec((tm, d), row)
    heads = pl.BlockSpec((tm, N_PAIRS, PAIR), lambda i: (i, 0, 0))
    tab = pl.BlockSpec((tm, PAIR), lambda i: (i % n_tab, 0))
    weight = lambda r, c: pl.BlockSpec((None, r, c), lambda i: (layer, 0, 0), pipeline_mode=pl.Buffered(1))
    f32_heads = jax.ShapeDtypeStruct((n, N_PAIRS, PAIR), F32)
    bf_out = jax.ShapeDtypeStruct((n, WIDTH), BF16)
    bf_full = jax.ShapeDtypeStruct((n, d), BF16)
    in_specs = [full, pl.BlockSpec((None, 1, d), lambda i: (layer, 0, 0)), weight(d, cols), tab, tab]
    args = [x, norm_w, w_in_bf, cos_tab, sin_tab]
    if wkv_t_bf is not None:
        in_specs.append(weight(2 * WIDTH, d))
        args.append(wkv_t_bf)
        sb_spec = pl.BlockSpec((None, WIDTH, tm), lambda i: (i // n_tab, 0, i % n_tab))
        sb_f32 = jax.ShapeDtypeStruct((n // seq, WIDTH, seq), F32)
        sb_bf = jax.ShapeDtypeStruct((n // seq, WIDTH, seq), BF16)
    else:
        sb_spec = wide
        sb_f32 = jax.ShapeDtypeStruct((n, WIDTH), F32)
        sb_bf = bf_out
    return pl.pallas_call(
        functools.partial(_inproj_kernel, feature_major_sb=wkv_t_bf is not None),
        grid=(n // tm,),
        in_specs=in_specs,
        out_specs=[wide, heads, heads, wide, wide, sb_spec, sb_spec, wide, full, full,
                   wide, wide, sb_spec, sb_spec],
        out_shape=[bf_out, f32_heads, f32_heads, bf_out, bf_out, sb_f32, sb_f32, bf_out, bf_full, bf_full,
                   bf_out, bf_out, sb_bf, sb_bf],
        compiler_params=pltpu.CompilerParams(dimension_semantics=("parallel",),
                                             vmem_limit_bytes=VMEM_LIMIT),
        name="inproj",
    )(*args)


def _post_kernel(x_ref, yda_ref, ysb_ref, ga_ref, gb_ref, wda_ref, wsb_ref, wout_ref, fnw_ref, o_ref,
                 *, final_norm):
    pda = _dot(yda_ref[...], wda_ref[...])
    psb = _dot(ysb_ref[...], wsb_ref[...])
    merged = (ga_ref[...].astype(F32) * pda + gb_ref[...].astype(F32) * psb).astype(BF16)
    out = x_ref[...] + _dot(merged, wout_ref[...])
    if final_norm:
        ms = jnp.mean(out * out, axis=-1, keepdims=True)
        out = out * lax.rsqrt(ms + EPS) * fnw_ref[...]
    o_ref[...] = out


def _post(x, yda, ysb, ga, gb, wda_bf, wsb_bf, wout_bf, final_norm_w, layer, final_norm):
    n, d = x.shape
    tm = TOKEN_TILE
    row = lambda i: (i, 0)
    wide = pl.BlockSpec((tm, WIDTH), row)
    full = pl.BlockSpec((tm, d), row)
    weight = lambda r: pl.BlockSpec((None, r, d), lambda i: (layer, 0, 0), pipeline_mode=pl.Buffered(1))
    return pl.pallas_call(
        functools.partial(_post_kernel, final_norm=final_norm),
        grid=(n // tm,),
        in_specs=[full, wide, wide, full, full, weight(WIDTH), weight(WIDTH), weight(d),
                  pl.BlockSpec((1, d), lambda i: (0, 0))],
        out_specs=full,
        out_shape=jax.ShapeDtypeStruct((n, d), F32),
        compiler_params=pltpu.CompilerParams(dimension_semantics=("parallel",),
                                             vmem_limit_bytes=VMEM_LIMIT),
        name="post",
    )(x, yda, ysb, ga, gb, wda_bf, wsb_bf, wout_bf, final_norm_w)


def _split_pair(q):
    lane = lax.broadcasted_iota(jnp.int32, q.shape, 1)
    first = lane < HEAD_DIM
    zero = jnp.zeros_like(q)
    return jnp.concatenate([jnp.where(first, q, zero), jnp.where(first, zero, q)], axis=0)


def _lambda(lam_ref, lam_init):
    lf = lam_ref[...]
    a = jnp.sum(lf[0:1] * lf[1:2], axis=-1, keepdims=True)
    b = jnp.sum(lf[2:3] * lf[3:4], axis=-1, keepdims=True)
    return jnp.exp(a) - jnp.exp(b) + lam_init


def _softmax_step(s, v, carry):
    m, l, acc = carry
    m_new = jnp.maximum(m, jnp.max(s, axis=-1, keepdims=True))
    alpha = jnp.exp(m - m_new)
    p = jnp.exp(s - m_new)
    l = alpha * l + jnp.sum(p, axis=-1, keepdims=True)
    acc = alpha * acc + _dot(p.astype(BF16), v)
    return m_new, l, acc


def _da_finish(carry, lam, sw, gate, lam_init):
    _, l, acc = carry
    t = acc.shape[0] // 2
    o = acc[:t] * (1.0 / l[:t]) - lam * (acc[t:] * (1.0 / l[t:]))
    ms = jnp.mean(o * o, axis=-1, keepdims=True)
    y = (o * lax.rsqrt(ms + EPS) * sw) * (1.0 - lam_init)
    return (y * gate.astype(F32)).astype(BF16)


def _tri(n):
    j = lax.broadcasted_iota(jnp.int32, (n, n), 0)
    s = lax.broadcasted_iota(jnp.int32, (n, n), 1)
    return (j >= s).astype(BF16)


def _sb_step(z, pv, tri, carry, mask=None):
    run, acc = carry
    lf = -(jnp.maximum(z, 0.0) + jnp.log(1.0 + jnp.exp(-jnp.abs(z))))
    if mask is not None:
        lf = jnp.where(mask, lf, 0.0)
    hi = lf.astype(BF16)
    lo = (lf - hi.astype(F32)).astype(BF16)
    c = _dot(hi, tri) + _dot(lo, tri) + run
    a = jnp.exp(z + c)
    if mask is not None:
        a = jnp.where(mask, a, 0.0)
    run = run + jnp.sum(lf, axis=-1, keepdims=True)
    acc = acc + pv(a.astype(BF16))
    return run, acc


def _sb_finish(carry, gate):
    _, acc = carry
    t = acc.shape[0] // 2
    lane = lax.broadcasted_iota(jnp.int32, (t, PAIR), 1)
    o = jnp.where(lane < HEAD_DIM, acc[:t], acc[t:])
    return (o * gate.astype(F32)).astype(BF16)


def _da_prompt_kernel(lam_ref, sw_ref, q_ref, k_ref, v_ref, g_ref, o_ref, *, lam_init):
    i = pl.program_id(1)
    tq = q_ref.shape[0]
    lam = _lambda(lam_ref, lam_init)
    sw = sw_ref[...]
    row = lax.broadcasted_iota(jnp.int32, (2 * tq, tq), 0) % tq
    col = lax.broadcasted_iota(jnp.int32, (2 * tq, tq), 1)
    diag_mask = (col // CHUNK) <= (row // CHUNK)
    diag = pl.multiple_of(i * tq, tq)
    for h in range(N_PAIRS):
        sl = slice(h * PAIR, (h + 1) * PAIR)
        q2 = _split_pair(q_ref[:, sl])
        init = (jnp.full((2 * tq, 1), NEG_INF, F32), jnp.zeros((2 * tq, 1), F32),
                jnp.zeros((2 * tq, PAIR), F32))
        s = jnp.where(diag_mask, _nt_dot(q2, k_ref[pl.ds(diag, tq), sl]), NEG_INF)
        carry = _softmax_step(s, v_ref[pl.ds(diag, tq), sl], init)

        def body(kb, carry):
            start = pl.multiple_of(kb * tq, tq)
            return _softmax_step(_nt_dot(q2, k_ref[pl.ds(start, tq), sl]), v_ref[pl.ds(start, tq), sl], carry)

        carry = lax.fori_loop(0, i, body, carry)
        o_ref[:, sl] = _da_finish(carry, lam, sw, g_ref[:, sl], lam_init)


def _sb_prompt_kernel(q_ref, kt_ref, vt_ref, g_ref, o_ref):
    i = pl.program_id(1)
    tq = q_ref.shape[0]
    row = lax.broadcasted_iota(jnp.int32, (2 * tq, tq), 0) % tq
    col = lax.broadcasted_iota(jnp.int32, (2 * tq, tq), 1)
    diag_mask = col < row
    tri = _tri(tq)
    diag = pl.multiple_of(i * tq, tq)
    for h in range(N_PAIRS):
        sl = slice(h * PAIR, (h + 1) * PAIR)
        q2 = _split_pair(q_ref[:, sl])

        def block(start, carry, mask=None):
            z = _dot(q2, kt_ref[sl, pl.ds(start, tq)])
            return _sb_step(z, lambda a: _nt_dot(a, vt_ref[sl, pl.ds(start, tq)]), tri, carry, mask)

        init = (jnp.zeros((2 * tq, 1), F32), jnp.zeros((2 * tq, PAIR), F32))
        carry = block(diag, init, diag_mask)
        carry = lax.fori_loop(0, i, lambda j, c: block(pl.multiple_of((i - 1 - j) * tq, tq), c), carry)
        o_ref[:, sl] = _sb_finish(carry, g_ref[:, sl])


def _prompt_attention(kernel_fn, name, batch, seq, q, k, v, g, feature_major_kv,
                      prefix_args=(), prefix_specs=()):
    tq = ATTN_TILE
    qspec = pl.BlockSpec((None, tq, WIDTH), lambda b, i: (b, i, 0))
    shape3 = lambda a: a.reshape(batch, seq, WIDTH)
    if feature_major_kv:
        kvspec = pl.BlockSpec((None, WIDTH, seq), lambda b, i: (b, 0, 0))
    else:
        kvspec = pl.BlockSpec((None, seq, WIDTH), lambda b, i: (b, 0, 0))
        k, v = shape3(k), shape3(v)
    out = pl.pallas_call(
        kernel_fn,
        grid=(batch, seq // tq),
        in_specs=list(prefix_specs) + [qspec, kvspec, kvspec, qspec],
        out_specs=qspec,
        out_shape=jax.ShapeDtypeStruct((batch, seq, WIDTH), BF16),
        compiler_params=pltpu.CompilerParams(dimension_semantics=("parallel", "arbitrary"),
                                             vmem_limit_bytes=VMEM_LIMIT),
        name=name,
    )(*prefix_args, shape3(q), k, v, shape3(g))
    return out.reshape(batch * seq, WIDTH)


def _pad_rows(a, rows):
    return jnp.concatenate([a, jnp.zeros((rows - a.shape[0], a.shape[1]), a.dtype)], axis=0)


def _da_decode_kernel(lam_ref, sw_ref, q_ref, ck_ref, cv_ref, nk_ref, nv_ref, g_ref, o_ref,
                      m_ref, l_ref, acc_ref, *, lam_init, past_len):
    j = pl.program_id(1)
    t = q_ref.shape[0]

    @pl.when(j == 0)
    def _():
        m_ref[...] = jnp.full(m_ref.shape, NEG_INF, F32)
        l_ref[...] = jnp.zeros(l_ref.shape, F32)
        acc_ref[...] = jnp.zeros(acc_ref.shape, F32)

    for h in range(N_PAIRS):
        q2 = _split_pair(q_ref[:, h * PAIR:(h + 1) * PAIR])
        carry = (m_ref[h], l_ref[h], acc_ref[h])
        m, l, acc = _softmax_step(_nt_dot(q2, ck_ref[:, h, :].astype(BF16)), cv_ref[:, h, :].astype(BF16), carry)
        m_ref[h] = m
        l_ref[h] = l
        acc_ref[h] = acc

    @pl.when(j == pl.num_programs(1) - 1)
    def _():
        lam = _lambda(lam_ref, lam_init)
        sw = sw_ref[...]
        q_pos = past_len + lax.broadcasted_iota(jnp.int32, (2 * t, PAIR), 0) % t
        k_idx = lax.broadcasted_iota(jnp.int32, (2 * t, PAIR), 1)
        mask = (k_idx < t) & (((past_len + k_idx) // CHUNK) <= (q_pos // CHUNK))
        for h in range(N_PAIRS):
            sl = slice(h * PAIR, (h + 1) * PAIR)
            q2 = _split_pair(q_ref[:, sl])
            s = jnp.where(mask, _nt_dot(q2, _pad_rows(nk_ref[:, sl], PAIR)), NEG_INF)
            carry = _softmax_step(s, _pad_rows(nv_ref[:, sl], PAIR), (m_ref[h], l_ref[h], acc_ref[h]))
            o_ref[:, sl] = _da_finish(carry, lam, sw, g_ref[:, sl], lam_init)


def _sb_decode_kernel(q_ref, ckt_ref, cvt_ref, nk_ref, nv_ref, g_ref, o_ref, run_ref, acc_ref):
    j = pl.program_id(1)
    t = q_ref.shape[0]
    sub = ATTN_TILE
    n_sub = ckt_ref.shape[1] // sub
    tri = _tri(sub)

    @pl.when(j == 0)
    def _():
        q_idx = lax.broadcasted_iota(jnp.int32, (2 * t, PAIR), 0) % t
        k_idx = lax.broadcasted_iota(jnp.int32, (2 * t, PAIR), 1)
        mask = k_idx < q_idx
        for h in range(N_PAIRS):
            sl = slice(h * PAIR, (h + 1) * PAIR)
            q2 = _split_pair(q_ref[:, sl])
            nv = _pad_rows(nv_ref[:, sl], PAIR)
            init = (jnp.zeros((2 * t, 1), F32), jnp.zeros((2 * t, PAIR), F32))
            run, acc = _sb_step(_nt_dot(q2, _pad_rows(nk_ref[:, sl], PAIR)), lambda a: _dot(a, nv),
                                tri[:PAIR, :PAIR], init, mask)
            run_ref[h] = run
            acc_ref[h] = acc

    for h in range(N_PAIRS):
        sl = slice(h * PAIR, (h + 1) * PAIR)
        q2 = _split_pair(q_ref[:, sl])
        carry = (run_ref[h], acc_ref[h])
        for c in reversed(range(n_sub)):
            cols = slice(c * sub, (c + 1) * sub)
            z = _dot(q2, ckt_ref[sl, cols].astype(BF16))
            carry = _sb_step(z, lambda a: _nt_dot(a, cvt_ref[sl, cols].astype(BF16)), tri, carry)
        run_ref[h] = carry[0]
        acc_ref[h] = carry[1]

    @pl.when(j == pl.num_programs(1) - 1)
    def _():
        for h in range(N_PAIRS):
            sl = slice(h * PAIR, (h + 1) * PAIR)
            o_ref[:, sl] = _sb_finish((run_ref[h], acc_ref[h]), g_ref[:, sl])


def _decode_attention(kernel_fn, name, t, q, cache_k, cache_v, cache_spec, nk, nv, g, scratch,
                      prefix_args=(), prefix_specs=()):
    batch = cache_k.shape[1]
    tok = pl.BlockSpec((None, t, WIDTH), lambda b, j: (b, 0, 0))
    shape3 = lambda a: a.reshape(batch, t, WIDTH)
    out = pl.pallas_call(
        kernel_fn,
        grid=(batch, cache_k.size // (cache_k.shape[0] * batch * WIDTH * CACHE_TILE)),
        in_specs=list(prefix_specs) + [tok, cache_spec, cache_spec, tok, tok, tok],
        out_specs=tok,
        out_shape=jax.ShapeDtypeStruct((batch, t, WIDTH), BF16),
        scratch_shapes=scratch,
        compiler_params=pltpu.CompilerParams(dimension_semantics=("parallel", "arbitrary"),
                                             vmem_limit_bytes=VMEM_LIMIT),
        name=name,
    )(*prefix_args, shape3(q), cache_k, cache_v, shape3(nk), shape3(nv), shape3(g))
    return out.reshape(batch * t, WIDTH)


def _rope_tables(pos):
    half = HEAD_DIM // 2
    inv = ROPE_THETA ** (-jnp.arange(half, dtype=F32) / half)
    ang = pos.astype(F32)[:, None] * inv[None, :]
    cos = jnp.cos(ang)
    sin = jnp.sin(ang)
    reps = PAIR // HEAD_DIM
    return (jnp.tile(jnp.concatenate([cos, cos], axis=1), (1, reps)),
            jnp.tile(jnp.concatenate([-sin, sin], axis=1), (1, reps)))


def kernel(x_prompt, x_sample, cache_da_k, cache_da_v, cache_sb_k, cache_sb_v, norm_w, w_in, lambda_qk,
           subln_w, w_proj_da, w_proj_sb, w_out, final_norm_w):
    batch, seq, d = x_prompt.shape
    dec_batch, dec_seq, _ = x_sample.shape
    depth, _, past_len, da_heads, _ = cache_da_k.shape
    sb_heads = cache_sb_k.shape[3]
    assert seq % ATTN_TILE == 0 and ATTN_TILE % CHUNK == 0 and ATTN_TILE == TOKEN_TILE
    assert dec_batch * dec_seq == TOKEN_TILE and past_len % CACHE_TILE == 0 and dec_seq <= PAIR
    assert da_heads == N_PAIRS and cache_da_k.shape[4] == PAIR and sb_heads * HEAD_DIM == WIDTH

    cos_p, sin_p = _rope_tables(jnp.arange(seq, dtype=jnp.int32))
    cos_s, sin_s = _rope_tables(jnp.tile(past_len + jnp.arange(dec_seq, dtype=jnp.int32), dec_batch))

    w_in_bf = w_in.astype(BF16)
    wkv_t_bf = jnp.swapaxes(w_in[:, :, G_KS * WIDTH:(G_VS + 1) * WIDTH], 1, 2).astype(BF16)
    wda_bf = w_proj_da.astype(BF16)
    wsb_bf = w_proj_sb.astype(BF16)
    wout_bf = w_out.astype(BF16)
    norm_w3 = norm_w.reshape(depth, 1, d)
    subln_w3 = subln_w.reshape(depth, 1, PAIR)
    fnw = final_norm_w.reshape(1, d)
    feature_major = lambda c: jnp.transpose(c, (0, 1, 3, 4, 2)).reshape(depth, dec_batch, WIDTH, past_len)
    cskt, csvt = feature_major(cache_sb_k), feature_major(cache_sb_v)
    n_blocks = past_len // CACHE_TILE

    xp = x_prompt.reshape(batch * seq, d)
    xs = x_sample.reshape(dec_batch * dec_seq, d)
    rows_p, rows_s = [], []
    for layer in range(depth):
        lam_init = 0.8 - 0.6 * math.exp(-0.3 * layer)
        lam_args = (lambda_qk, subln_w3)
        lam_specs = (pl.BlockSpec((None,) + lambda_qk.shape[1:], lambda b, i, layer=layer: (layer, 0, 0)),
                     pl.BlockSpec((None, 1, PAIR), lambda b, i, layer=layer: (layer, 0, 0)))
        last = layer == depth - 1

        (qd, kd, vd, gd, qs, kst, vst, gs, ga, gb, kdb, vdb, kstb, vstb) = _inproj(
            xp, norm_w3, w_in_bf, layer, cos_p, sin_p, wkv_t_bf)
        yda = _prompt_attention(functools.partial(_da_prompt_kernel, lam_init=lam_init), "da_prompt",
                                batch, seq, qd, kdb, vdb, gd, False, lam_args, lam_specs)
        ysb = _prompt_attention(_sb_prompt_kernel, "sb_prompt", batch, seq, qs, kstb, vstb, gs, True)
        xp = _post(xp, yda, ysb, ga, gb, wda_bf, wsb_bf, wout_bf, fnw, layer, last)
        rows_p.append((kd, vd, kst, vst))

        (qd, kd, vd, gd, qs, ks, vs, gs, ga, gb, kdb, vdb, ksb, vsb) = _inproj(
            xs, norm_w3, w_in_bf, layer, cos_s, sin_s)
        state = (N_PAIRS, 2 * dec_seq)
        yda = _decode_attention(
            functools.partial(_da_decode_kernel, lam_init=lam_init, past_len=past_len), "da_decode",
            dec_seq, qd, cache_da_k, cache_da_v,
            pl.BlockSpec((None, None, CACHE_TILE, N_PAIRS, PAIR), lambda b, j, layer=layer: (layer, b, j, 0, 0)),
            kdb, vdb, gd,
            [pltpu.VMEM(state + (1,), F32), pltpu.VMEM(state + (1,), F32), pltpu.VMEM(state + (PAIR,), F32)],
            lam_args, lam_specs)
        ysb = _decode_attention(
            _sb_decode_kernel, "sb_decode", dec_seq, qs, cskt, csvt,
            pl.BlockSpec((None, None, WIDTH, CACHE_TILE),
                         lambda b, j, layer=layer: (layer, b, 0, n_blocks - 1 - j)),
            ksb, vsb, gs,
            [pltpu.VMEM(state + (1,), F32), pltpu.VMEM(state + (PAIR,), F32)])
        xs = _post(xs, yda, ysb, ga, gb, wda_bf, wsb_bf, wout_bf, fnw, layer, last)
        rows_s.append((kd, vd, ks, vs))

    def stack(rows, idx, shape):
        return jnp.stack([r[idx] for r in rows], axis=0).reshape((depth,) + shape)

    def stack_feature_major(rows, idx):
        a = stack(rows, idx, (batch, sb_heads, HEAD_DIM, seq))
        return jnp.transpose(a, (0, 1, 4, 2, 3))

    da_p, sb_s = (batch, seq, da_heads, PAIR), (dec_batch, dec_seq, sb_heads, HEAD_DIM)
    da_s = (dec_batch, dec_seq, da_heads, PAIR)
    return (xp.reshape(batch, seq, d), xs.reshape(dec_batch, dec_seq, d),
            stack(rows_p, 0, da_p), stack(rows_p, 1, da_p),
            stack_feature_major(rows_p, 2), stack_feature_major(rows_p, 3),
            stack(rows_s, 0, da_s), stack(rows_s, 1, da_s), stack(rows_s, 2, sb_s), stack(rows_s, 3, sb_s))
```

```python
import functools
import math

import jax
import jax.numpy as jnp
import numpy as np
from jax import lax
from jax.experimental import pallas as pl
from jax.experimental.pallas import tpu as pltpu

F32 = jnp.float32
BF16 = jnp.bfloat16

CHUNK = 64
HEAD_DIM = 64
PAIR = 2 * HEAD_DIM
N_PAIRS = 4
WIDTH = N_PAIRS * PAIR
ROPE_THETA = 10000.0
EPS = 1e-6
NEG_INF = -1e30
QK_SCALE = HEAD_DIM ** -0.5
LOG2E = math.log2(math.e)
SIGN_BIT = np.uint32(0x80000000)
DEAD_LOG2 = -150.0

TOKEN_TILE = 256
ATTN_TILE = 256
CACHE_TILE = 1024
VMEM_LIMIT = 56 * 1024 * 1024

G_QD, G_KD, G_VD, G_GD, G_QS, G_KS, G_VS, G_GS, G_GA, G_GB = 0, 1, 2, 3, 4, 5, 6, 7, 8, 10


def _sigmoid(x):
    return 1.0 / (1.0 + jnp.exp(-x))


def _nt_dot(a, b):
    return lax.dot_general(a, b, (((1,), (1,)), ((), ())), preferred_element_type=F32)


def _dot(a, b):
    return jnp.dot(a, b, preferred_element_type=F32)


def _inproj_kernel(*refs, feature_major_sb):
    if feature_major_sb:
        x_ref, nw_ref, w_ref, cos_ref, sin_ref, wkv_ref = refs[:6]
        outs = refs[6:]
    else:
        x_ref, nw_ref, w_ref, cos_ref, sin_ref = refs[:5]
        outs = refs[5:]
    (qd_ref, kd_ref, vd_ref, gd_ref, qs_ref, ks_ref, vs_ref, gs_ref, ga_ref, gb_ref,
     kdb_ref, vdb_ref, ksb_ref, vsb_ref) = outs

    x = x_ref[...]
    ms = jnp.mean(x * x, axis=-1, keepdims=True)
    h = (x * lax.rsqrt(ms + EPS) * nw_ref[...]).astype(BF16)

    def proj(group):
        return _dot(h, w_ref[:, group * WIDTH:(group + 1) * WIDTH])

    lane = lax.broadcasted_iota(jnp.int32, (1, PAIR), 1)
    upper_half = (lane & (HEAD_DIM // 2)) != 0
    cos = cos_ref[...]
    sin = sin_ref[...]

    def rope(s):
        partner = jnp.where(upper_half, pltpu.roll(s, HEAD_DIM // 2, 1),
                            pltpu.roll(s, PAIR - HEAD_DIM // 2, 1))
        return s * cos + partner * sin

    def slab(u, c):
        return u[:, c * PAIR:(c + 1) * PAIR]

    u = proj(G_QD)
    for c in range(N_PAIRS):
        qd_ref[:, c * PAIR:(c + 1) * PAIR] = (rope(slab(u, c)) * QK_SCALE).astype(BF16)
    u = proj(G_KD)
    for c in range(N_PAIRS):
        kd = rope(slab(u, c))
        kd_ref[:, c, :] = kd
        kdb_ref[:, c * PAIR:(c + 1) * PAIR] = kd.astype(BF16)
    u = proj(G_VD)
    for c in range(N_PAIRS):
        vd_ref[:, c, :] = slab(u, c)
    vdb_ref[...] = u.astype(BF16)
    g = proj(G_GD)
    gd_ref[...] = (g * _sigmoid(g)).astype(BF16)
    qs_ref[...] = (proj(G_QS) * QK_SCALE).astype(BF16)
    if feature_major_sb:
        for half, (o_ref, ob_ref) in enumerate(((ks_ref, ksb_ref), (vs_ref, vsb_ref))):
            ut = _nt_dot(wkv_ref[half * WIDTH:(half + 1) * WIDTH, :], h)
            o_ref[...] = ut
            ob_ref[...] = ut.astype(BF16)
    else:
        for group, o_ref, ob_ref in ((G_KS, ks_ref, ksb_ref), (G_VS, vs_ref, vsb_ref)):
            u = proj(group)
            o_ref[...] = u
            ob_ref[...] = u.astype(BF16)
    g = proj(G_GS)
    gs_ref[...] = (g * _sigmoid(g)).astype(BF16)
    for half in range(2):
        ga_ref[:, half * WIDTH:(half + 1) * WIDTH] = _sigmoid(proj(G_GA + half)).astype(BF16)
        gb_ref[:, half * WIDTH:(half + 1) * WIDTH] = _sigmoid(proj(G_GB + half)).astype(BF16)


def _inproj(x, norm_w, w_in_bf, layer, cos_tab, sin_tab, wkv_t_bf=None):
    n, d = x.shape
    tm = TOKEN_TILE
    seq = cos_tab.shape[0]
    n_tab = seq // tm
    cols = w_in_bf.shape[-1]
    row = lambda i: (i, 0)
    wide = pl.BlockSpec((tm, WIDTH), row)
    full = pl.BlockSpec((tm, d), row)
    heads = pl.BlockSpec((tm, N_PAIRS, PAIR), lambda i: (i, 0, 0))
    tab = pl.BlockSpec((tm, PAIR), lambda i: (i % n_tab, 0))
    weight = lambda r, c: pl.BlockSpec((None, r, c), lambda i: (layer, 0, 0), pipeline_mode=pl.Buffered(1))
    f32_heads = jax.ShapeDtypeStruct((n, N_PAIRS, PAIR), F32)
    bf_out = jax.ShapeDtypeStruct((n, WIDTH), BF16)
    bf_full = jax.ShapeDtypeStruct((n, d), BF16)
    in_specs = [full, pl.BlockSpec((None, 1, d), lambda i: (layer, 0, 0)), weight(d, cols), tab, tab]
    args = [x, norm_w, w_in_bf, cos_tab, sin_tab]
    if wkv_t_bf is not None:
        in_specs.append(weight(2 * WIDTH, d))
        args.append(wkv_t_bf)
        sb_spec = pl.BlockSpec((None, WIDTH, tm), lambda i: (i // n_tab, 0, i % n_tab))
        sb_f32 = jax.ShapeDtypeStruct((n // seq, WIDTH, seq), F32)
        sb_bf = jax.ShapeDtypeStruct((n // seq, WIDTH, seq), BF16)
    else:
        sb_spec = wide
        sb_f32 = jax.ShapeDtypeStruct((n, WIDTH), F32)
        sb_bf = bf_out
    return pl.pallas_call(
        functools.partial(_inproj_kernel, feature_major_sb=wkv_t_bf is not None),
        grid=(n // tm,),
        in_specs=in_specs,
        out_specs=[wide, heads, heads, wide, wide, sb_spec, sb_spec, wide, full, full,
                   wide, wide, sb_spec, sb_spec],
        out_shape=[bf_out, f32_heads, f32_heads, bf_out, bf_out, sb_f32, sb_f32, bf_out, bf_full, bf_full,
                   bf_out, bf_out, sb_bf, sb_bf],
        compiler_params=pltpu.CompilerParams(dimension_semantics=("parallel",),
                                             vmem_limit_bytes=VMEM_LIMIT),
        name="inproj",
    )(*args)


def _post_kernel(x_ref, yda_ref, ysb_ref, ga_ref, gb_ref, wda_ref, wsb_ref, wout_ref, fnw_ref, o_ref,
                 *, final_norm):
    pda = _dot(yda_ref[...], wda_ref[...])
    psb = _dot(ysb_ref[...], wsb_ref[...])
    merged = (ga_ref[...].astype(F32) * pda + gb_ref[...].astype(F32) * psb).astype(BF16)
    out = x_ref[...] + _dot(merged, wout_ref[...])
    if final_norm:
        ms = jnp.mean(out * out, axis=-1, keepdims=True)
        out = out * lax.rsqrt(ms + EPS) * fnw_ref[...]
    o_ref[...] = out


def _post(x, yda, ysb, ga, gb, wda_bf, wsb_bf, wout_bf, final_norm_w, layer, final_norm):
    n, d = x.shape
    tm = TOKEN_TILE
    row = lambda i: (i, 0)
    wide = pl.BlockSpec((tm, WIDTH), row)
    full = pl.BlockSpec((tm, d), row)
    weight = lambda r: pl.BlockSpec((None, r, d), lambda i: (layer, 0, 0), pipeline_mode=pl.Buffered(1))
    return pl.pallas_call(
        functools.partial(_post_kernel, final_norm=final_norm),
        grid=(n // tm,),
        in_specs=[full, wide, wide, full, full, weight(WIDTH), weight(WIDTH), weight(d),
                  pl.BlockSpec((1, d), lambda i: (0, 0))],
        out_specs=full,
        out_shape=jax.ShapeDtypeStruct((n, d), F32),
        compiler_params=pltpu.CompilerParams(dimension_semantics=("parallel",),
                                             vmem_limit_bytes=VMEM_LIMIT),
        name="post",
    )(x, yda, ysb, ga, gb, wda_bf, wsb_bf, wout_bf, final_norm_w)


def _split_pair(q):
    lane = lax.broadcasted_iota(jnp.int32, q.shape, 1)
    first = lane < HEAD_DIM
    zero = jnp.zeros_like(q)
    return jnp.concatenate([jnp.where(first, q, zero), jnp.where(first, zero, q)], axis=0)


def _lambda(lam_ref, lam_init):
    lf = lam_ref[...]
    a = jnp.sum(lf[0:1] * lf[1:2], axis=-1, keepdims=True)
    b = jnp.sum(lf[2:3] * lf[3:4], axis=-1, keepdims=True)
    return jnp.exp(a) - jnp.exp(b) + lam_init


def _softmax_step(s, v, carry):
    m, l, acc = carry
    m_new = jnp.maximum(m, jnp.max(s, axis=-1, keepdims=True))
    alpha = jnp.exp(m - m_new)
    p = jnp.exp(s - m_new)
    l = alpha * l + jnp.sum(p, axis=-1, keepdims=True)
    acc = alpha * acc + _dot(p.astype(BF16), v)
    return m_new, l, acc


def _da_finish(carry, lam, sw, gate, lam_init):
    _, l, acc = carry
    t = acc.shape[0] // 2
    o = acc[:t] * (1.0 / l[:t]) - lam * (acc[t:] * (1.0 / l[t:]))
    return _da_gate(o, sw, gate, lam_init)


def _da_gate(o, sw, gate, lam_init):
    ms = jnp.mean(o * o, axis=-1, keepdims=True)
    y = (o * lax.rsqrt(ms + EPS) * sw) * (1.0 - lam_init)
    return (y * gate.astype(F32)).astype(BF16)


def _neg_tri(n):
    j = lax.broadcasted_iota(jnp.int32, (n, n), 0)
    s = lax.broadcasted_iota(jnp.int32, (n, n), 1)
    return jnp.where(j >= s, -1.0, 0.0).astype(BF16)


def _sb_step(z, pv, neg_tri, carry, mask=None):
    run, acc = carry
    z2, hi, lo, total = _sb_front(z, mask)
    return run - total, acc + pv(_sb_back(z2, hi, lo, run, neg_tri, mask))


def _sb_front(z, mask=None):
    z2 = z * LOG2E
    neg_abs = lax.bitcast_convert_type(lax.bitcast_convert_type(z2, jnp.uint32) | SIGN_BIT, F32)
    sp = jnp.maximum(z2, 0.0) + jnp.log2(1.0 + jnp.exp2(neg_abs))
    if mask is not None:
        sp = jnp.where(mask, sp, 0.0)
    hi = sp.astype(BF16)
    lo = (sp - hi.astype(F32)).astype(BF16)
    return z2, hi, lo, jnp.sum(sp, axis=-1, keepdims=True)


def _sb_back(z2, hi, lo, run, neg_tri, mask=None):
    c = _dot(hi, neg_tri) + _dot(lo, neg_tri)
    if run.shape[1] != 1:
        run = jnp.concatenate([run] * (z2.shape[1] // run.shape[1]), axis=1)
    a = jnp.exp2(z2 + c + run)
    if mask is not None:
        a = jnp.where(mask, a, 0.0)
    return a.astype(BF16)


def _sb_alive(run):
    return (jnp.max(run) > DEAD_LOG2).astype(jnp.int32)


def _sb_finish(carry, gate):
    _, acc = carry
    t = acc.shape[0] // 2
    lane = lax.broadcasted_iota(jnp.int32, (t, PAIR), 1)
    o = jnp.where(lane < HEAD_DIM, acc[:t], acc[t:])
    return (o * gate.astype(F32)).astype(BF16)


def _pair(h):
    return slice(h * PAIR, (h + 1) * PAIR)


def _da_prompt_kernel(lam_ref, sw_ref, q_ref, k_ref, v_ref, g_ref, o_ref, q2_ref, s_ref, mx_ref, acc_ref,
                      *, lam_init):
    i = pl.program_id(1)
    tq = q_ref.shape[0]
    row = lax.broadcasted_iota(jnp.int32, (2 * tq, tq), 0) % tq
    col = lax.broadcasted_iota(jnp.int32, (2 * tq, tq), 1)
    diag_mask = (col // CHUNK) <= (row // CHUNK)
    ones = jnp.ones((tq, PAIR), BF16)

    def scores(h, start, mask=None):
        s = _nt_dot(q2_ref[h], k_ref[pl.ds(start, tq), _pair(h)])
        if mask is not None:
            s = jnp.where(mask, s, NEG_INF)
        s_ref[h, :, pl.ds(start, tq)] = s
        return jnp.maximum(s[:, :PAIR], s[:, PAIR:])

    for h in range(N_PAIRS):
        q2_ref[h] = _split_pair(q_ref[:, _pair(h)])
    for h in range(N_PAIRS):
        mx_ref[h] = scores(h, pl.multiple_of(i * tq, tq), diag_mask)

    @pl.loop(0, i)
    def _(kb):
        for h in range(N_PAIRS):
            mx_ref[h] = jnp.maximum(mx_ref[h], scores(h, pl.multiple_of(kb * tq, tq)))

    for h in range(N_PAIRS):
        m2 = jnp.max(mx_ref[h], axis=-1, keepdims=True) * LOG2E
        mx_ref[h] = jnp.broadcast_to(m2, mx_ref.shape[1:])
    acc_ref[...] = jnp.zeros(acc_ref.shape, F32)

    @pl.loop(0, i + 1)
    def _(kb):
        start = pl.multiple_of(kb * tq, tq)
        for h in range(N_PAIRS):
            m2 = mx_ref[h]
            p = jnp.exp2(s_ref[h, :, pl.ds(start, tq)] * LOG2E - jnp.concatenate([m2, m2], axis=1))
            v_ones = jnp.concatenate([v_ref[pl.ds(start, tq), _pair(h)], ones], axis=1)
            acc_ref[h] += _dot(p.astype(BF16), v_ones)

    lam = _lambda(lam_ref, lam_init)
    sw = sw_ref[...]
    for h in range(N_PAIRS):
        acc = acc_ref[h]
        o = acc[:tq, :PAIR] / acc[:tq, PAIR:] - lam * (acc[tq:, :PAIR] / acc[tq:, PAIR:])
        o_ref[:, _pair(h)] = _da_gate(o, sw, g_ref[:, _pair(h)], lam_init)


def _sb_prompt_kernel(q_ref, kt_ref, vt_ref, g_ref, o_ref, q2_ref, z_ref, hi_ref, lo_ref, tot_ref, run_ref,
                      acc_ref):
    i = pl.program_id(1)
    tq = q_ref.shape[0]
    row = lax.broadcasted_iota(jnp.int32, (2 * tq, tq), 0) % tq
    col = lax.broadcasted_iota(jnp.int32, (2 * tq, tq), 1)
    diag_mask = col < row
    neg_tri = _neg_tri(tq)

    def block(start, mask=None):
        for h in range(N_PAIRS):
            z2, hi, lo, total = _sb_front(_dot(q2_ref[h], kt_ref[_pair(h), pl.ds(start, tq)]), mask)
            z_ref[h] = z2
            hi_ref[h] = hi
            lo_ref[h] = lo
            tot_ref[h] = jnp.broadcast_to(total, tot_ref.shape[1:])
        for h in range(N_PAIRS):
            a = _sb_back(z_ref[h], hi_ref[h], lo_ref[h], run_ref[h], neg_tri, mask)
            acc_ref[h] += _nt_dot(a, vt_ref[_pair(h), pl.ds(start, tq)])
            run_ref[h] = run_ref[h] - tot_ref[h]

    run_ref[...] = jnp.zeros(run_ref.shape, F32)
    acc_ref[...] = jnp.zeros(acc_ref.shape, F32)
    for h in range(N_PAIRS):
        q2_ref[h] = _split_pair(q_ref[:, _pair(h)])
    block(pl.multiple_of(i * tq, tq), diag_mask)

    def earlier_block(state):
        kb, _ = state
        block(pl.multiple_of(kb * tq, tq))
        return kb - 1, _sb_alive(run_ref[...])

    lax.while_loop(lambda state: (state[0] >= 0) & (state[1] > 0), earlier_block,
                   (i - 1, _sb_alive(run_ref[...])))
    for h in range(N_PAIRS):
        o_ref[:, _pair(h)] = _sb_finish((run_ref[h], acc_ref[h]), g_ref[:, _pair(h)])


def _prompt_attention(kernel_fn, name, batch, seq, q, k, v, g, feature_major_kv, scratch,
                      prefix_args=(), prefix_specs=()):
    tq = ATTN_TILE
    qspec = pl.BlockSpec((None, tq, WIDTH), lambda b, i: (b, i, 0))
    shape3 = lambda a: a.reshape(batch, seq, WIDTH)
    if feature_major_kv:
        kvspec = pl.BlockSpec((None, WIDTH, seq), lambda b, i: (b, 0, 0))
    else:
        kvspec = pl.BlockSpec((None, seq, WIDTH), lambda b, i: (b, 0, 0))
        k, v = shape3(k), shape3(v)
    out = pl.pallas_call(
        kernel_fn,
        grid=(batch, seq // tq),
        in_specs=list(prefix_specs) + [qspec, kvspec, kvspec, qspec],
        out_specs=qspec,
        out_shape=jax.ShapeDtypeStruct((batch, seq, WIDTH), BF16),
        scratch_shapes=scratch,
        compiler_params=pltpu.CompilerParams(dimension_semantics=("parallel", "arbitrary"),
                                             vmem_limit_bytes=VMEM_LIMIT),
        name=name,
    )(*prefix_args, shape3(q), k, v, shape3(g))
    return out.reshape(batch * seq, WIDTH)


def _pad_rows(a, rows):
    return jnp.concatenate([a, jnp.zeros((rows - a.shape[0], a.shape[1]), a.dtype)], axis=0)


def _da_decode_kernel(lam_ref, sw_ref, q_ref, ck_ref, cv_ref, nk_ref, nv_ref, g_ref, o_ref,
                      q_all_ref, m_ref, l_ref, acc_ref, *, lam_init, past_len):
    j = pl.program_id(1)
    t = q_ref.shape[0]
    n_rows = N_PAIRS * 2 * t
    q_head = lax.broadcasted_iota(jnp.int32, (n_rows, 1), 0) // (2 * t)

    @pl.when(j == 0)
    def _():
        m_ref[...] = jnp.full(m_ref.shape, NEG_INF, F32)
        l_ref[...] = jnp.zeros(l_ref.shape, F32)
        acc_ref[...] = jnp.zeros(acc_ref.shape, F32)
        q_all_ref[...] = jnp.concatenate([_split_pair(q_ref[:, _pair(h)]) for h in range(N_PAIRS)], axis=0)

    key_head = lax.broadcasted_iota(jnp.int32, (1, ck_ref.shape[0]), 1) % N_PAIRS
    s = jnp.where(key_head == q_head, _nt_dot(q_all_ref[...], ck_ref[...].astype(BF16)), NEG_INF)
    m, l, acc = _softmax_step(s, cv_ref[...].astype(BF16), (m_ref[...], l_ref[...], acc_ref[...]))
    m_ref[...] = m
    l_ref[...] = l
    acc_ref[...] = acc

    @pl.when(j == pl.num_programs(1) - 1)
    def _():
        lam = _lambda(lam_ref, lam_init)
        sw = sw_ref[...]
        stack = lambda ref: _pad_rows(jnp.concatenate([ref[:, _pair(h)] for h in range(N_PAIRS)], axis=0), PAIR)
        k_idx = lax.broadcasted_iota(jnp.int32, (1, PAIR), 1)
        q_pos = past_len + lax.broadcasted_iota(jnp.int32, (n_rows, 1), 0) % t
        mask = ((k_idx // t == q_head) & (((past_len + k_idx % t) // CHUNK) <= (q_pos // CHUNK)))
        s = jnp.where(mask, _nt_dot(q_all_ref[...], stack(nk_ref)), NEG_INF)
        m, l, acc = _softmax_step(s, stack(nv_ref), (m_ref[...], l_ref[...], acc_ref[...]))
        for h in range(N_PAIRS):
            rows = slice(h * 2 * t, (h + 1) * 2 * t)
            o_ref[:, _pair(h)] = _da_finish((m[rows], l[rows], acc[rows]), lam, sw, g_ref[:, _pair(h)], lam_init)


def _sb_decode_kernel(q_ref, ckt_ref, cvt_ref, nk_ref, nv_ref, g_ref, o_ref, run_ref, acc_ref):
    j = pl.program_id(1)
    t = q_ref.shape[0]
    sub = ATTN_TILE
    n_sub = ckt_ref.shape[1] // sub
    neg_tri = _neg_tri(sub)

    @pl.when(j == 0)
    def _():
        q_idx = lax.broadcasted_iota(jnp.int32, (2 * t, PAIR), 0) % t
        k_idx = lax.broadcasted_iota(jnp.int32, (2 * t, PAIR), 1)
        mask = k_idx < q_idx
        for h in range(N_PAIRS):
            sl = slice(h * PAIR, (h + 1) * PAIR)
            q2 = _split_pair(q_ref[:, sl])
            nv = _pad_rows(nv_ref[:, sl], PAIR)
            init = (jnp.zeros((2 * t, 1), F32), jnp.zeros((2 * t, PAIR), F32))
            run, acc = _sb_step(_nt_dot(q2, _pad_rows(nk_ref[:, sl], PAIR)), lambda a: _dot(a, nv),
                                neg_tri[:PAIR, :PAIR], init, mask)
            run_ref[h] = run
            acc_ref[h] = acc

    for c in reversed(range(n_sub)):
        cols = slice(c * sub, (c + 1) * sub)

        @pl.when(_sb_alive(run_ref[...]) > 0)
        def _():
            for h in range(N_PAIRS):
                sl = slice(h * PAIR, (h + 1) * PAIR)
                q2 = _split_pair(q_ref[:, sl])
                z = _dot(q2, ckt_ref[sl, cols].astype(BF16))
                run, acc = _sb_step(z, lambda a: _nt_dot(a, cvt_ref[sl, cols].astype(BF16)), neg_tri,
                                    (run_ref[h], acc_ref[h]))
                run_ref[h] = run
                acc_ref[h] = acc

    @pl.when(j == pl.num_programs(1) - 1)
    def _():
        for h in range(N_PAIRS):
            sl = slice(h * PAIR, (h + 1) * PAIR)
            o_ref[:, sl] = _sb_finish((run_ref[h], acc_ref[h]), g_ref[:, sl])


def _decode_attention(kernel_fn, name, t, q, cache_k, cache_v, cache_spec, nk, nv, g, scratch,
                      prefix_args=(), prefix_specs=()):
    batch = cache_k.shape[1]
    tok = pl.BlockSpec((None, t, WIDTH), lambda b, j: (b, 0, 0))
    shape3 = lambda a: a.reshape(batch, t, WIDTH)
    out = pl.pallas_call(
        kernel_fn,
        grid=(batch, cache_k.size // (cache_k.shape[0] * batch * WIDTH * CACHE_TILE)),
        in_specs=list(prefix_specs) + [tok, cache_spec, cache_spec, tok, tok, tok],
        out_specs=tok,
        out_shape=jax.ShapeDtypeStruct((batch, t, WIDTH), BF16),
        scratch_shapes=scratch,
        compiler_params=pltpu.CompilerParams(dimension_semantics=("parallel", "arbitrary"),
                                             vmem_limit_bytes=VMEM_LIMIT),
        name=name,
    )(*prefix_args, shape3(q), cache_k, cache_v, shape3(nk), shape3(nv), shape3(g))
    return out.reshape(batch * t, WIDTH)


def _rope_tables(pos):
    half = HEAD_DIM // 2
    inv = ROPE_THETA ** (-jnp.arange(half, dtype=F32) / half)
    ang = pos.astype(F32)[:, None] * inv[None, :]
    cos = jnp.cos(ang)
    sin = jnp.sin(ang)
    reps = PAIR // HEAD_DIM
    return (jnp.tile(jnp.concatenate([cos, cos], axis=1), (1, reps)),
            jnp.tile(jnp.concatenate([-sin, sin], axis=1), (1, reps)))


def kernel(x_prompt, x_sample, cache_da_k, cache_da_v, cache_sb_k, cache_sb_v, norm_w, w_in, lambda_qk,
           subln_w, w_proj_da, w_proj_sb, w_out, final_norm_w):
    batch, seq, d = x_prompt.shape
    dec_batch, dec_seq, _ = x_sample.shape
    depth, _, past_len, da_heads, _ = cache_da_k.shape
    sb_heads = cache_sb_k.shape[3]
    assert seq % ATTN_TILE == 0 and ATTN_TILE % CHUNK == 0 and ATTN_TILE == TOKEN_TILE
    assert dec_batch * dec_seq == TOKEN_TILE and past_len % CACHE_TILE == 0 and dec_seq <= PAIR
    assert da_heads == N_PAIRS and cache_da_k.shape[4] == PAIR and sb_heads * HEAD_DIM == WIDTH

    cos_p, sin_p = _rope_tables(jnp.arange(seq, dtype=jnp.int32))
    cos_s, sin_s = _rope_tables(jnp.tile(past_len + jnp.arange(dec_seq, dtype=jnp.int32), dec_batch))

    w_in_bf = w_in.astype(BF16)
    wkv_t_bf = jnp.swapaxes(w_in[:, :, G_KS * WIDTH:(G_VS + 1) * WIDTH], 1, 2).astype(BF16)
    wda_bf = w_proj_da.astype(BF16)
    wsb_bf = w_proj_sb.astype(BF16)
    wout_bf = w_out.astype(BF16)
    norm_w3 = norm_w.reshape(depth, 1, d)
    subln_w3 = subln_w.reshape(depth, 1, PAIR)
    fnw = final_norm_w.reshape(1, d)
    feature_major = lambda c: jnp.transpose(c, (0, 1, 3, 4, 2)).reshape(depth, dec_batch, WIDTH, past_len)
    cskt, csvt = feature_major(cache_sb_k), feature_major(cache_sb_v)
    head_rows = lambda c: c.reshape(depth, dec_batch, past_len * da_heads, PAIR)
    cdk, cdv = head_rows(cache_da_k), head_rows(cache_da_v)
    n_blocks = past_len // CACHE_TILE

    xp = x_prompt.reshape(batch * seq, d)
    xs = x_sample.reshape(dec_batch * dec_seq, d)
    rows_p, rows_s = [], []
    for layer in range(depth):
        lam_init = 0.8 - 0.6 * math.exp(-0.3 * layer)
        lam_args = (lambda_qk, subln_w3)
        lam_specs = (pl.BlockSpec((None,) + lambda_qk.shape[1:], lambda b, i, layer=layer: (layer, 0, 0)),
                     pl.BlockSpec((None, 1, PAIR), lambda b, i, layer=layer: (layer, 0, 0)))
        last = layer == depth - 1

        (qd, kd, vd, gd, qs, kst, vst, gs, ga, gb, kdb, vdb, kstb, vstb) = _inproj(
            xp, norm_w3, w_in_bf, layer, cos_p, sin_p, wkv_t_bf)
        rows2 = (N_PAIRS, 2 * ATTN_TILE)
        yda = _prompt_attention(
            functools.partial(_da_prompt_kernel, lam_init=lam_init), "da_prompt",
            batch, seq, qd, kdb, vdb, gd, False,
            [pltpu.VMEM(rows2 + (PAIR,), BF16), pltpu.VMEM(rows2 + (seq,), F32),
             pltpu.VMEM(rows2 + (PAIR,), F32), pltpu.VMEM(rows2 + (2 * PAIR,), F32)],
            lam_args, lam_specs)
        ysb = _prompt_attention(
            _sb_prompt_kernel, "sb_prompt", batch, seq, qs, kstb, vstb, gs, True,
            [pltpu.VMEM(rows2 + (PAIR,), BF16), pltpu.VMEM(rows2 + (ATTN_TILE,), F32),
             pltpu.VMEM(rows2 + (ATTN_TILE,), BF16), pltpu.VMEM(rows2 + (ATTN_TILE,), BF16),
             pltpu.VMEM(rows2 + (PAIR,), F32), pltpu.VMEM(rows2 + (PAIR,), F32),
             pltpu.VMEM(rows2 + (PAIR,), F32)])
        xp = _post(xp, yda, ysb, ga, gb, wda_bf, wsb_bf, wout_bf, fnw, layer, last)
        rows_p.append((kd, vd, kst, vst))

        (qd, kd, vd, gd, qs, ks, vs, gs, ga, gb, kdb, vdb, ksb, vsb) = _inproj(
            xs, norm_w3, w_in_bf, layer, cos_s, sin_s)
        state = (N_PAIRS, 2 * dec_seq)
        all_rows = N_PAIRS * 2 * dec_seq
        yda = _decode_attention(
            functools.partial(_da_decode_kernel, lam_init=lam_init, past_len=past_len), "da_decode",
            dec_seq, qd, cdk, cdv,
            pl.BlockSpec((None, None, CACHE_TILE * da_heads, PAIR), lambda b, j, layer=layer: (layer, b, j, 0)),
            kdb, vdb, gd,
            [pltpu.VMEM((all_rows, PAIR), BF16), pltpu.VMEM((all_rows, 1), F32),
             pltpu.VMEM((all_rows, 1), F32), pltpu.VMEM((all_rows, PAIR), F32)],
            lam_args, lam_specs)
        ysb = _decode_attention(
            _sb_decode_kernel, "sb_decode", dec_seq, qs, cskt, csvt,
            pl.BlockSpec((None, None, WIDTH, CACHE_TILE),
                         lambda b, j, layer=layer: (layer, b, 0, n_blocks - 1 - j)),
            ksb, vsb, gs,
            [pltpu.VMEM(state + (1,), F32), pltpu.VMEM(state + (PAIR,), F32)])
        xs = _post(xs, yda, ysb, ga, gb, wda_bf, wsb_bf, wout_bf, fnw, layer, last)
        rows_s.append((kd, vd, ks, vs))

    def stack(rows, idx, shape):
        return jnp.stack([r[idx] for r in rows], axis=0).reshape((depth,) + shape)

    def stack_feature_major(rows, idx):
        a = stack(rows, idx, (batch, sb_heads, HEAD_DIM, seq))
        return jnp.transpose(a, (0, 1, 4, 2, 3))

    da_p, sb_s = (batch, seq, da_heads, PAIR), (dec_batch, dec_seq, sb_heads, HEAD_DIM)
    da_s = (dec_batch, dec_seq, da_heads, PAIR)
    return (xp.reshape(batch, seq, d), xs.reshape(dec_batch, dec_seq, d),
            stack(rows_p, 0, da_p), stack(rows_p, 1, da_p),
            stack_feature_major(rows_p, 2), stack_feature_major(rows_p, 3),
            stack(rows_s, 0, da_s), stack(rows_s, 1, da_s), stack(rows_s, 2, sb_s), stack(rows_s, 3, sb_s))
```

```python
import functools
import math

import jax
import jax.numpy as jnp
import numpy as np
from jax import lax
from jax.experimental import pallas as pl
from jax.experimental.pallas import tpu as pltpu

F32 = jnp.float32
BF16 = jnp.bfloat16

CHUNK = 64
HEAD_DIM = 64
PAIR = 2 * HEAD_DIM
N_PAIRS = 4
WIDTH = N_PAIRS * PAIR
ROPE_THETA = 10000.0
EPS = 1e-6
NEG_INF = -1e30
QK_SCALE = HEAD_DIM ** -0.5
LOG2E = math.log2(math.e)
SIGN_BIT = np.uint32(0x80000000)
DEAD_LOG = -150.0 * math.log(2.0)

TOKEN_TILE = 512
ATTN_TILE = 256
CACHE_TILE = 1024
VMEM_LIMIT = 56 * 1024 * 1024

G_QD, G_KD, G_VD, G_GD, G_QS, G_KS, G_VS, G_GS, G_GA, G_GB = 0, 1, 2, 3, 4, 5, 6, 7, 8, 10
KV_OUTPUTS = (1, 2, 5, 6)


def _sigmoid(x):
    return 1.0 / (1.0 + jnp.exp(-x))


def _nt_dot(a, b):
    return lax.dot_general(a, b, (((1,), (1,)), ((), ())), preferred_element_type=F32)


def _dot(a, b):
    return jnp.dot(a, b, preferred_element_type=F32)


def _inproj_kernel(*refs, feature_major_sb, n_aliased):
    if feature_major_sb:
        x_ref, nw_ref, w_ref, cos_ref, sin_ref, wkv_ref = refs[:6]
        outs = refs[6 + n_aliased:]
    else:
        x_ref, nw_ref, w_ref, cos_ref, sin_ref = refs[:5]
        outs = refs[5 + n_aliased:]
    (qd_ref, kd_ref, vd_ref, gd_ref, qs_ref, ks_ref, vs_ref, gs_ref, ga_ref, gb_ref,
     kdb_ref, vdb_ref, ksb_ref, vsb_ref) = outs

    x = x_ref[...]
    ms = jnp.mean(x * x, axis=-1, keepdims=True)
    h = (x * lax.rsqrt(ms + EPS) * nw_ref[...]).astype(BF16)

    def proj(group):
        return _dot(h, w_ref[:, group * WIDTH:(group + 1) * WIDTH])

    lane = lax.broadcasted_iota(jnp.int32, (1, PAIR), 1)
    upper_half = (lane & (HEAD_DIM // 2)) != 0
    cos = cos_ref[...]
    sin = sin_ref[...]

    def rope(s):
        partner = jnp.where(upper_half, pltpu.roll(s, HEAD_DIM // 2, 1),
                            pltpu.roll(s, PAIR - HEAD_DIM // 2, 1))
        return s * cos + partner * sin

    def slab(u, c):
        return u[:, c * PAIR:(c + 1) * PAIR]

    u = proj(G_QD)
    for c in range(N_PAIRS):
        qd_ref[:, c * PAIR:(c + 1) * PAIR] = (rope(slab(u, c)) * QK_SCALE).astype(BF16)
    u = proj(G_KD)
    for c in range(N_PAIRS):
        kd = rope(slab(u, c))
        kd_ref[:, c, :] = kd
        kdb_ref[:, c * PAIR:(c + 1) * PAIR] = kd.astype(BF16)
    u = proj(G_VD)
    for c in range(N_PAIRS):
        vd_ref[:, c, :] = slab(u, c)
    vdb_ref[...] = u.astype(BF16)
    g = proj(G_GD)
    gd_ref[...] = (g * _sigmoid(g)).astype(BF16)
    qs_ref[...] = (proj(G_QS) * QK_SCALE).astype(BF16)
    if feature_major_sb:
        for half, (o_ref, ob_ref) in enumerate(((ks_ref, ksb_ref), (vs_ref, vsb_ref))):
            ut = _nt_dot(wkv_ref[half * WIDTH:(half + 1) * WIDTH, :], h)
            o_ref[...] = ut
            ob_ref[...] = ut.astype(BF16)
    else:
        for group, o_ref, ob_ref in ((G_KS, ks_ref, ksb_ref), (G_VS, vs_ref, vsb_ref)):
            u = proj(group)
            o_ref[...] = u
            ob_ref[...] = u.astype(BF16)
    g = proj(G_GS)
    gs_ref[...] = (g * _sigmoid(g)).astype(BF16)
    for half in range(2):
        ga_ref[:, half * WIDTH:(half + 1) * WIDTH] = _sigmoid(proj(G_GA + half)).astype(BF16)
        gb_ref[:, half * WIDTH:(half + 1) * WIDTH] = _sigmoid(proj(G_GB + half)).astype(BF16)


def _inproj(x, norm_w, w_in_bf, layer, cos_tab, sin_tab, wkv_t_bf=None, kv_stack=None):
    n, d = x.shape
    tm = min(TOKEN_TILE, n)
    seq = cos_tab.shape[0]
    n_tab = seq // tm
    depth, _, cols = w_in_bf.shape
    row = lambda i: (i, 0)
    wide = pl.BlockSpec((tm, WIDTH), row)
    full = pl.BlockSpec((tm, d), row)
    tab = pl.BlockSpec((tm, PAIR), lambda i: (i % n_tab, 0))
    weight = lambda r, c: pl.BlockSpec((None, r, c), lambda i: (layer, 0, 0), pipeline_mode=pl.Buffered(1))
    bf_out = jax.ShapeDtypeStruct((n, WIDTH), BF16)
    bf_full = jax.ShapeDtypeStruct((n, d), BF16)
    in_specs = [full, pl.BlockSpec((None, 1, d), lambda i: (layer, 0, 0)), weight(d, cols), tab, tab]
    args = [x, norm_w, w_in_bf, cos_tab, sin_tab]
    aliases = {}
    if wkv_t_bf is not None:
        in_specs.append(weight(2 * WIDTH, d))
        args.append(wkv_t_bf)
        heads = pl.BlockSpec((None, tm, N_PAIRS, PAIR), lambda i: (layer, i, 0, 0))
        f32_heads = jax.ShapeDtypeStruct((depth, n, N_PAIRS, PAIR), F32)
        sb_spec = pl.BlockSpec((None, None, WIDTH, tm), lambda i: (layer, i // n_tab, 0, i % n_tab))
        sb_f32 = jax.ShapeDtypeStruct((depth, n // seq, WIDTH, seq), F32)
        sb_bf_spec = pl.BlockSpec((None, WIDTH, tm), lambda i: (i // n_tab, 0, i % n_tab))
        sb_bf = jax.ShapeDtypeStruct((n // seq, WIDTH, seq), BF16)
        if kv_stack is not None:
            for out_index, a in zip(KV_OUTPUTS, kv_stack):
                aliases[len(args)] = out_index
                in_specs.append(pl.BlockSpec(memory_space=pl.ANY))
                args.append(a)
    else:
        heads = pl.BlockSpec((tm, N_PAIRS, PAIR), lambda i: (i, 0, 0))
        f32_heads = jax.ShapeDtypeStruct((n, N_PAIRS, PAIR), F32)
        sb_spec = sb_bf_spec = wide
        sb_f32 = jax.ShapeDtypeStruct((n, WIDTH), F32)
        sb_bf = bf_out
    return pl.pallas_call(
        functools.partial(_inproj_kernel, feature_major_sb=wkv_t_bf is not None, n_aliased=len(aliases)),
        grid=(n // tm,),
        in_specs=in_specs,
        out_specs=[wide, heads, heads, wide, wide, sb_spec, sb_spec, wide, full, full,
                   wide, wide, sb_bf_spec, sb_bf_spec],
        out_shape=[bf_out, f32_heads, f32_heads, bf_out, bf_out, sb_f32, sb_f32, bf_out, bf_full, bf_full,
                   bf_out, bf_out, sb_bf, sb_bf],
        input_output_aliases=aliases,
        compiler_params=pltpu.CompilerParams(dimension_semantics=("parallel",),
                                             vmem_limit_bytes=VMEM_LIMIT),
        name="inproj",
    )(*args)


def _post_kernel(x_ref, yda_ref, ysb_ref, ga_ref, gb_ref, wda_ref, wsb_ref, wout_ref, fnw_ref, o_ref,
                 *, final_norm):
    pda = _dot(yda_ref[...], wda_ref[...])
    psb = _dot(ysb_ref[...], wsb_ref[...])
    merged = (ga_ref[...].astype(F32) * pda + gb_ref[...].astype(F32) * psb).astype(BF16)
    out = x_ref[...] + _dot(merged, wout_ref[...])
    if final_norm:
        ms = jnp.mean(out * out, axis=-1, keepdims=True)
        out = out * lax.rsqrt(ms + EPS) * fnw_ref[...]
    o_ref[...] = out


def _post(x, yda, ysb, ga, gb, wda_bf, wsb_bf, wout_bf, final_norm_w, layer, final_norm):
    n, d = x.shape
    tm = min(TOKEN_TILE, n)
    row = lambda i: (i, 0)
    wide = pl.BlockSpec((tm, WIDTH), row)
    full = pl.BlockSpec((tm, d), row)
    weight = lambda r: pl.BlockSpec((None, r, d), lambda i: (layer, 0, 0), pipeline_mode=pl.Buffered(1))
    return pl.pallas_call(
        functools.partial(_post_kernel, final_norm=final_norm),
        grid=(n // tm,),
        in_specs=[full, wide, wide, full, full, weight(WIDTH), weight(WIDTH), weight(d),
                  pl.BlockSpec((1, d), lambda i: (0, 0))],
        out_specs=full,
        out_shape=jax.ShapeDtypeStruct((n, d), F32),
        compiler_params=pltpu.CompilerParams(dimension_semantics=("parallel",),
                                             vmem_limit_bytes=VMEM_LIMIT),
        name="post",
    )(x, yda, ysb, ga, gb, wda_bf, wsb_bf, wout_bf, final_norm_w)


def _split_pair(q):
    lane = lax.broadcasted_iota(jnp.int32, q.shape, 1)
    first = lane < HEAD_DIM
    zero = jnp.zeros_like(q)
    return jnp.concatenate([jnp.where(first, q, zero), jnp.where(first, zero, q)], axis=0)


def _lambda(lam_ref, lam_init):
    lf = lam_ref[...]
    a = jnp.sum(lf[0:1] * lf[1:2], axis=-1, keepdims=True)
    b = jnp.sum(lf[2:3] * lf[3:4], axis=-1, keepdims=True)
    return jnp.exp(a) - jnp.exp(b) + lam_init


def _softmax_step(s, v, carry):
    m, l, acc = carry
    m_new = jnp.maximum(m, jnp.max(s, axis=-1, keepdims=True))
    alpha = jnp.exp(m - m_new)
    p = jnp.exp(s - m_new)
    l = alpha * l + jnp.sum(p, axis=-1, keepdims=True)
    acc = alpha * acc + _dot(p.astype(BF16), v)
    return m_new, l, acc


def _da_finish(carry, lam, sw, gate, lam_init):
    _, l, acc = carry
    t = acc.shape[0] // 2
    o = acc[:t] * (1.0 / l[:t]) - lam * (acc[t:] * (1.0 / l[t:]))
    return _da_gate(o, sw, gate, lam_init)


def _da_gate(o, sw, gate, lam_init):
    ms = jnp.mean(o * o, axis=-1, keepdims=True)
    y = (o * lax.rsqrt(ms + EPS) * sw) * (1.0 - lam_init)
    return (y * gate.astype(F32)).astype(BF16)


def _neg_tri(n):
    j = lax.broadcasted_iota(jnp.int32, (n, n), 0)
    s = lax.broadcasted_iota(jnp.int32, (n, n), 1)
    return jnp.where(j >= s, -1.0, 0.0).astype(BF16)


def _sb_step(z, pv, neg_tri, carry, mask=None):
    run, acc = carry
    hi, lo, total = _sb_front(z, mask)
    c = _dot(hi, neg_tri) + _dot(lo, neg_tri)
    return run - total, acc + pv(_sb_back(z, c, run, mask))


def _sb_front(z, mask=None):
    neg_abs = lax.bitcast_convert_type(lax.bitcast_convert_type(z * LOG2E, jnp.uint32) | SIGN_BIT, F32)
    sp = jnp.maximum(z, 0.0) + jnp.log(1.0 + jnp.exp2(neg_abs))
    if mask is not None:
        sp = jnp.where(mask, sp, 0.0)
    hi = sp.astype(BF16)
    lo = (sp - hi.astype(F32)).astype(BF16)
    return hi, lo, jnp.sum(sp, axis=-1, keepdims=True)


def _sb_back(z, c, run, mask=None):
    if run.shape[1] != 1:
        run = jnp.concatenate([run] * (z.shape[1] // run.shape[1]), axis=1)
    a = jnp.exp2((z + c + run) * LOG2E)
    if mask is not None:
        a = jnp.where(mask, a, 0.0)
    return a.astype(BF16)


def _sb_alive(run):
    return (jnp.max(run) > DEAD_LOG).astype(jnp.int32)


def _sb_finish(carry, gate):
    _, acc = carry
    t = acc.shape[0] // 2
    lane = lax.broadcasted_iota(jnp.int32, (t, PAIR), 1)
    o = jnp.where(lane < HEAD_DIM, acc[:t], acc[t:])
    return (o * gate.astype(F32)).astype(BF16)


def _pair(h):
    return slice(h * PAIR, (h + 1) * PAIR)


def _da_prompt_kernel(lam_ref, sw_ref, q_ref, k_ref, v_ref, g_ref, o_ref, q2_ref, s_ref, mx_ref, acc_ref,
                      *, lam_init):
    i = pl.program_id(1)
    tq = q_ref.shape[0]
    row = lax.broadcasted_iota(jnp.int32, (2 * tq, tq), 0) % tq
    col = lax.broadcasted_iota(jnp.int32, (2 * tq, tq), 1)
    diag_mask = (col // CHUNK) <= (row // CHUNK)
    ones = jnp.ones((tq, PAIR), BF16)

    def scores(h, start, mask=None):
        s = _nt_dot(q2_ref[h], k_ref[pl.ds(start, tq), _pair(h)]) * LOG2E
        if mask is not None:
            s = jnp.where(mask, s, NEG_INF)
        s_ref[h, :, pl.ds(start, tq)] = s
        return jnp.maximum(s[:, :PAIR], s[:, PAIR:])

    for h in range(N_PAIRS):
        q2_ref[h] = _split_pair(q_ref[:, _pair(h)])
    for h in range(N_PAIRS):
        mx_ref[h] = scores(h, pl.multiple_of(i * tq, tq), diag_mask)

    @pl.loop(0, i)
    def _(kb):
        for h in range(N_PAIRS):
            mx_ref[h] = jnp.maximum(mx_ref[h], scores(h, pl.multiple_of(kb * tq, tq)))

    for h in range(N_PAIRS):
        m2 = jnp.max(mx_ref[h], axis=-1, keepdims=True)
        mx_ref[h] = jnp.broadcast_to(m2, mx_ref.shape[1:])
    acc_ref[...] = jnp.zeros(acc_ref.shape, F32)

    @pl.loop(0, i + 1)
    def _(kb):
        start = pl.multiple_of(kb * tq, tq)
        for h in range(N_PAIRS):
            m2 = mx_ref[h]
            p = jnp.exp2(s_ref[h, :, pl.ds(start, tq)] - jnp.concatenate([m2, m2], axis=1))
            v_ones = jnp.concatenate([v_ref[pl.ds(start, tq), _pair(h)], ones], axis=1)
            acc_ref[h] += _dot(p.astype(BF16), v_ones)

    lam = _lambda(lam_ref, lam_init)
    sw = sw_ref[...]
    for h in range(N_PAIRS):
        acc = acc_ref[h]
        o = acc[:tq, :PAIR] / acc[:tq, PAIR:] - lam * (acc[tq:, :PAIR] / acc[tq:, PAIR:])
        o_ref[:, _pair(h)] = _da_gate(o, sw, g_ref[:, _pair(h)], lam_init)


def _sb_prompt_kernel(q_ref, kt_ref, vt_ref, g_ref, o_ref, q2_ref, z_ref, hi_ref, lo_ref, c_ref, tot_ref, run_ref,
                      acc_ref):
    i = pl.program_id(1)
    tq = q_ref.shape[0]
    row = lax.broadcasted_iota(jnp.int32, (2 * tq, tq), 0) % tq
    col = lax.broadcasted_iota(jnp.int32, (2 * tq, tq), 1)
    diag_mask = col < row
    neg_tri = _neg_tri(tq)
    rows = lambda h: slice(h * 2 * tq, (h + 1) * 2 * tq)
    once = jnp.minimum(i, 0) + 1

    def block(start, mask=None):
        @pl.loop(0, once)
        def _(_):
            for h in range(N_PAIRS):
                z = _dot(q2_ref[h], kt_ref[_pair(h), pl.ds(start, tq)])
                hi, lo, total = _sb_front(z, mask)
                z_ref[h] = z
                hi_ref[rows(h), :] = hi
                lo_ref[rows(h), :] = lo
                tot_ref[h] = jnp.broadcast_to(total, tot_ref.shape[1:])

        @pl.loop(0, once)
        def _(_):
            c_ref[...] = _dot(hi_ref[...], neg_tri) + _dot(lo_ref[...], neg_tri)

        @pl.loop(0, once)
        def _(_):
            for h in range(N_PAIRS):
                a = _sb_back(z_ref[h], c_ref[rows(h), :], run_ref[h], mask)
                acc_ref[h] += _nt_dot(a, vt_ref[_pair(h), pl.ds(start, tq)])
                run_ref[h] = run_ref[h] - tot_ref[h]

    run_ref[...] = jnp.zeros(run_ref.shape, F32)
    acc_ref[...] = jnp.zeros(acc_ref.shape, F32)
    for h in range(N_PAIRS):
        q2_ref[h] = _split_pair(q_ref[:, _pair(h)])
    block(pl.multiple_of(i * tq, tq), diag_mask)

    def earlier_block(state):
        kb, _ = state
        block(pl.multiple_of(kb * tq, tq))
        return kb - 1, _sb_alive(run_ref[...])

    lax.while_loop(lambda state: (state[0] >= 0) & (state[1] > 0), earlier_block,
                   (i - 1, _sb_alive(run_ref[...])))
    for h in range(N_PAIRS):
        o_ref[:, _pair(h)] = _sb_finish((run_ref[h], acc_ref[h]), g_ref[:, _pair(h)])


def _prompt_attention(kernel_fn, name, batch, seq, q, k, v, g, feature_major_kv, scratch,
                      prefix_args=(), prefix_specs=()):
    tq = ATTN_TILE
    qspec = pl.BlockSpec((None, tq, WIDTH), lambda b, i: (b, i, 0))
    shape3 = lambda a: a.reshape(batch, seq, WIDTH)
    if feature_major_kv:
        kvspec = pl.BlockSpec((None, WIDTH, seq), lambda b, i: (b, 0, 0))
    else:
        kvspec = pl.BlockSpec((None, seq, WIDTH), lambda b, i: (b, 0, 0))
        k, v = shape3(k), shape3(v)
    out = pl.pallas_call(
        kernel_fn,
        grid=(batch, seq // tq),
        in_specs=list(prefix_specs) + [qspec, kvspec, kvspec, qspec],
        out_specs=qspec,
        out_shape=jax.ShapeDtypeStruct((batch, seq, WIDTH), BF16),
        scratch_shapes=scratch,
        compiler_params=pltpu.CompilerParams(dimension_semantics=("parallel", "arbitrary"),
                                             vmem_limit_bytes=VMEM_LIMIT),
        name=name,
    )(*prefix_args, shape3(q), k, v, shape3(g))
    return out.reshape(batch * seq, WIDTH)


def _pad_rows(a, rows):
    return jnp.concatenate([a, jnp.zeros((rows - a.shape[0], a.shape[1]), a.dtype)], axis=0)


def _da_decode_kernel(lam_ref, sw_ref, q_ref, ck_ref, cv_ref, nk_ref, nv_ref, g_ref, o_ref,
                      q_all_ref, m_ref, l_ref, acc_ref, *, lam_init, past_len):
    j = pl.program_id(1)
    t = q_ref.shape[0]
    n_rows = N_PAIRS * 2 * t
    q_head = lax.broadcasted_iota(jnp.int32, (n_rows, 1), 0) // (2 * t)

    @pl.when(j == 0)
    def _():
        m_ref[...] = jnp.full(m_ref.shape, NEG_INF, F32)
        l_ref[...] = jnp.zeros(l_ref.shape, F32)
        acc_ref[...] = jnp.zeros(acc_ref.shape, F32)
        q_all_ref[...] = jnp.concatenate([_split_pair(q_ref[:, _pair(h)]) for h in range(N_PAIRS)], axis=0)

    key_head = lax.broadcasted_iota(jnp.int32, (1, ck_ref.shape[0]), 1) % N_PAIRS
    s = jnp.where(key_head == q_head, _nt_dot(q_all_ref[...], ck_ref[...].astype(BF16)), NEG_INF)
    m, l, acc = _softmax_step(s, cv_ref[...].astype(BF16), (m_ref[...], l_ref[...], acc_ref[...]))
    m_ref[...] = m
    l_ref[...] = l
    acc_ref[...] = acc

    @pl.when(j == pl.num_programs(1) - 1)
    def _():
        lam = _lambda(lam_ref, lam_init)
        sw = sw_ref[...]
        stack = lambda ref: _pad_rows(jnp.concatenate([ref[:, _pair(h)] for h in range(N_PAIRS)], axis=0), PAIR)
        k_idx = lax.broadcasted_iota(jnp.int32, (1, PAIR), 1)
        q_pos = past_len + lax.broadcasted_iota(jnp.int32, (n_rows, 1), 0) % t
        mask = ((k_idx // t == q_head) & (((past_len + k_idx % t) // CHUNK) <= (q_pos // CHUNK)))
        s = jnp.where(mask, _nt_dot(q_all_ref[...], stack(nk_ref)), NEG_INF)
        m, l, acc = _softmax_step(s, stack(nv_ref), (m_ref[...], l_ref[...], acc_ref[...]))
        for h in range(N_PAIRS):
            rows = slice(h * 2 * t, (h + 1) * 2 * t)
            o_ref[:, _pair(h)] = _da_finish((m[rows], l[rows], acc[rows]), lam, sw, g_ref[:, _pair(h)], lam_init)


def _sb_decode_kernel(q_ref, ckt_ref, cvt_ref, nk_ref, nv_ref, g_ref, o_ref, run_ref, acc_ref):
    j = pl.program_id(1)
    t = q_ref.shape[0]
    sub = ATTN_TILE
    n_sub = ckt_ref.shape[1] // sub
    neg_tri = _neg_tri(sub)

    @pl.when(j == 0)
    def _():
        q_idx = lax.broadcasted_iota(jnp.int32, (2 * t, PAIR), 0) % t
        k_idx = lax.broadcasted_iota(jnp.int32, (2 * t, PAIR), 1)
        mask = k_idx < q_idx
        for h in range(N_PAIRS):
            sl = slice(h * PAIR, (h + 1) * PAIR)
            q2 = _split_pair(q_ref[:, sl])
            nv = _pad_rows(nv_ref[:, sl], PAIR)
            init = (jnp.zeros((2 * t, 1), F32), jnp.zeros((2 * t, PAIR), F32))
            run, acc = _sb_step(_nt_dot(q2, _pad_rows(nk_ref[:, sl], PAIR)), lambda a: _dot(a, nv),
                                neg_tri[:PAIR, :PAIR], init, mask)
            run_ref[h] = run
            acc_ref[h] = acc

    for c in reversed(range(n_sub)):
        cols = slice(c * sub, (c + 1) * sub)

        @pl.when(_sb_alive(run_ref[...]) > 0)
        def _():
            for h in range(N_PAIRS):
                sl = slice(h * PAIR, (h + 1) * PAIR)
                q2 = _split_pair(q_ref[:, sl])
                z = _dot(q2, ckt_ref[sl, cols].astype(BF16))
                run, acc = _sb_step(z, lambda a: _nt_dot(a, cvt_ref[sl, cols].astype(BF16)), neg_tri,
                                    (run_ref[h], acc_ref[h]))
                run_ref[h] = run
                acc_ref[h] = acc

    @pl.when(j == pl.num_programs(1) - 1)
    def _():
        for h in range(N_PAIRS):
            sl = slice(h * PAIR, (h + 1) * PAIR)
            o_ref[:, sl] = _sb_finish((run_ref[h], acc_ref[h]), g_ref[:, sl])


def _decode_attention(kernel_fn, name, t, q, cache_k, cache_v, cache_spec, nk, nv, g, scratch,
                      prefix_args=(), prefix_specs=()):
    batch = cache_k.shape[1]
    tok = pl.BlockSpec((None, t, WIDTH), lambda b, j: (b, 0, 0))
    shape3 = lambda a: a.reshape(batch, t, WIDTH)
    out = pl.pallas_call(
        kernel_fn,
        grid=(batch, cache_k.size // (cache_k.shape[0] * batch * WIDTH * CACHE_TILE)),
        in_specs=list(prefix_specs) + [tok, cache_spec, cache_spec, tok, tok, tok],
        out_specs=tok,
        out_shape=jax.ShapeDtypeStruct((batch, t, WIDTH), BF16),
        scratch_shapes=scratch,
        compiler_params=pltpu.CompilerParams(dimension_semantics=("parallel", "arbitrary"),
                                             vmem_limit_bytes=VMEM_LIMIT),
        name=name,
    )(*prefix_args, shape3(q), cache_k, cache_v, shape3(nk), shape3(nv), shape3(g))
    return out.reshape(batch * t, WIDTH)


def _rope_tables(pos):
    half = HEAD_DIM // 2
    inv = ROPE_THETA ** (-jnp.arange(half, dtype=F32) / half)
    ang = pos.astype(F32)[:, None] * inv[None, :]
    cos = jnp.cos(ang)
    sin = jnp.sin(ang)
    reps = PAIR // HEAD_DIM
    return (jnp.tile(jnp.concatenate([cos, cos], axis=1), (1, reps)),
            jnp.tile(jnp.concatenate([-sin, sin], axis=1), (1, reps)))


def kernel(x_prompt, x_sample, cache_da_k, cache_da_v, cache_sb_k, cache_sb_v, norm_w, w_in, lambda_qk,
           subln_w, w_proj_da, w_proj_sb, w_out, final_norm_w):
    batch, seq, d = x_prompt.shape
    dec_batch, dec_seq, _ = x_sample.shape
    depth, _, past_len, da_heads, _ = cache_da_k.shape
    sb_heads = cache_sb_k.shape[3]
    assert seq % ATTN_TILE == 0 and ATTN_TILE % CHUNK == 0 and seq % TOKEN_TILE == 0
    assert dec_batch * dec_seq <= TOKEN_TILE and past_len % CACHE_TILE == 0 and N_PAIRS * dec_seq <= PAIR
    assert da_heads == N_PAIRS and cache_da_k.shape[4] == PAIR and sb_heads * HEAD_DIM == WIDTH

    cos_p, sin_p = _rope_tables(jnp.arange(seq, dtype=jnp.int32))
    cos_s, sin_s = _rope_tables(jnp.tile(past_len + jnp.arange(dec_seq, dtype=jnp.int32), dec_batch))

    w_in_bf = w_in.astype(BF16)
    wkv_t_bf = jnp.swapaxes(w_in[:, :, G_KS * WIDTH:(G_VS + 1) * WIDTH], 1, 2).astype(BF16)
    wda_bf = w_proj_da.astype(BF16)
    wsb_bf = w_proj_sb.astype(BF16)
    wout_bf = w_out.astype(BF16)
    norm_w3 = norm_w.reshape(depth, 1, d)
    subln_w3 = subln_w.reshape(depth, 1, PAIR)
    fnw = final_norm_w.reshape(1, d)
    feature_major = lambda c: jnp.transpose(c, (0, 1, 3, 4, 2)).reshape(depth, dec_batch, WIDTH, past_len)
    cskt, csvt = feature_major(cache_sb_k), feature_major(cache_sb_v)
    head_rows = lambda c: c.reshape(depth, dec_batch, past_len * da_heads, PAIR)
    cdk, cdv = head_rows(cache_da_k), head_rows(cache_da_v)
    n_blocks = past_len // CACHE_TILE

    xp = x_prompt.reshape(batch * seq, d)
    xs = x_sample.reshape(dec_batch * dec_seq, d)
    kv_stack, rows_s = None, []
    for layer in range(depth):
        lam_init = 0.8 - 0.6 * math.exp(-0.3 * layer)
        lam_args = (lambda_qk, subln_w3)
        lam_specs = (pl.BlockSpec((None,) + lambda_qk.shape[1:], lambda b, i, layer=layer: (layer, 0, 0)),
                     pl.BlockSpec((None, 1, PAIR), lambda b, i, layer=layer: (layer, 0, 0)))
        last = layer == depth - 1

        (qd, kd, vd, gd, qs, kst, vst, gs, ga, gb, kdb, vdb, kstb, vstb) = _inproj(
            xp, norm_w3, w_in_bf, layer, cos_p, sin_p, wkv_t_bf, kv_stack)
        kv_stack = (kd, vd, kst, vst)
        rows2 = (N_PAIRS, 2 * ATTN_TILE)
        stacked_rows = (N_PAIRS * 2 * ATTN_TILE, ATTN_TILE)
        yda = _prompt_attention(
            functools.partial(_da_prompt_kernel, lam_init=lam_init), "da_prompt",
            batch, seq, qd, kdb, vdb, gd, False,
            [pltpu.VMEM(rows2 + (PAIR,), BF16), pltpu.VMEM(rows2 + (seq,), F32),
             pltpu.VMEM(rows2 + (PAIR,), F32), pltpu.VMEM(rows2 + (2 * PAIR,), F32)],
            lam_args, lam_specs)
        ysb = _prompt_attention(
            _sb_prompt_kernel, "sb_prompt", batch, seq, qs, kstb, vstb, gs, True,
            [pltpu.VMEM(rows2 + (PAIR,), BF16), pltpu.VMEM(rows2 + (ATTN_TILE,), F32),
             pltpu.VMEM(stacked_rows, BF16), pltpu.VMEM(stacked_rows, BF16), pltpu.VMEM(stacked_rows, F32),
             pltpu.VMEM(rows2 + (PAIR,), F32), pltpu.VMEM(rows2 + (PAIR,), F32),
             pltpu.VMEM(rows2 + (PAIR,), F32)])
        xp = _post(xp, yda, ysb, ga, gb, wda_bf, wsb_bf, wout_bf, fnw, layer, last)

        (qd, kd, vd, gd, qs, ks, vs, gs, ga, gb, kdb, vdb, ksb, vsb) = _inproj(
            xs, norm_w3, w_in_bf, layer, cos_s, sin_s)
        state = (N_PAIRS, 2 * dec_seq)
        all_rows = N_PAIRS * 2 * dec_seq
        yda = _decode_attention(
            functools.partial(_da_decode_kernel, lam_init=lam_init, past_len=past_len), "da_decode",
            dec_seq, qd, cdk, cdv,
            pl.BlockSpec((None, None, CACHE_TILE * da_heads, PAIR), lambda b, j, layer=layer: (layer, b, j, 0)),
            kdb, vdb, gd,
            [pltpu.VMEM((all_rows, PAIR), BF16), pltpu.VMEM((all_rows, 1), F32),
             pltpu.VMEM((all_rows, 1), F32), pltpu.VMEM((all_rows, PAIR), F32)],
            lam_args, lam_specs)
        ysb = _decode_attention(
            _sb_decode_kernel, "sb_decode", dec_seq, qs, cskt, csvt,
            pl.BlockSpec((None, None, WIDTH, CACHE_TILE),
                         lambda b, j, layer=layer: (layer, b, 0, n_blocks - 1 - j)),
            ksb, vsb, gs,
            [pltpu.VMEM(state + (1,), F32), pltpu.VMEM(state + (PAIR,), F32)])
        xs = _post(xs, yda, ysb, ga, gb, wda_bf, wsb_bf, wout_bf, fnw, layer, last)
        rows_s.append((kd, vd, ks, vs))

    def stack(rows, idx, shape):
        return jnp.stack([r[idx] for r in rows], axis=0).reshape((depth,) + shape)

    def token_major(a):
        return jnp.transpose(a.reshape(depth, batch, sb_heads, HEAD_DIM, seq), (0, 1, 4, 2, 3))

    sb_s = (dec_batch, dec_seq, sb_heads, HEAD_DIM)
    da_s = (dec_batch, dec_seq, da_heads, PAIR)
    kd, vd, kst, vst = kv_stack
    return (xp.reshape(batch, seq, d), xs.reshape(dec_batch, dec_seq, d),
            kd.reshape(depth, batch, seq, da_heads, PAIR), vd.reshape(depth, batch, seq, da_heads, PAIR),
            token_major(kst), token_major(vst),
            stack(rows_s, 0, da_s), stack(rows_s, 1, da_s), stack(rows_s, 2, sb_s), stack(rows_s, 3, sb_s))
```

```python
import functools
import math

import jax
import jax.numpy as jnp
import numpy as np
from jax import lax
from jax.experimental import pallas as pl
from jax.experimental.pallas import tpu as pltpu

F32 = jnp.float32
BF16 = jnp.bfloat16

CHUNK = 64
HEAD_DIM = 64
PAIR = 2 * HEAD_DIM
N_PAIRS = 4
WIDTH = N_PAIRS * PAIR
ROPE_THETA = 10000.0
EPS = 1e-6
NEG_INF = -1e30
QK_SCALE = HEAD_DIM ** -0.5
LOG2E = math.log2(math.e)
SIGN_BIT = np.uint32(0x80000000)
DEAD_LOG = -150.0 * math.log(2.0)

TOKEN_TILE = 512
ATTN_TILE = 256
CACHE_TILE = 1024
VMEM_LIMIT = 56 * 1024 * 1024

G_QD, G_KD, G_VD, G_GD, G_QS, G_KS, G_VS, G_GS, G_GA, G_GB = 0, 1, 2, 3, 4, 5, 6, 7, 8, 10
KV_OUTPUTS = (1, 2, 5, 6)


def _sigmoid(x):
    return 1.0 / (1.0 + jnp.exp(-x))


def _nt_dot(a, b):
    return lax.dot_general(a, b, (((1,), (1,)), ((), ())), preferred_element_type=F32)


def _dot(a, b):
    return jnp.dot(a, b, preferred_element_type=F32)


def _inproj_kernel(*refs, feature_major_sb, n_aliased):
    if feature_major_sb:
        x_ref, nw_ref, w_ref, cos_ref, sin_ref, wkv_ref = refs[:6]
        outs = refs[6 + n_aliased:]
    else:
        x_ref, nw_ref, w_ref, cos_ref, sin_ref = refs[:5]
        outs = refs[5 + n_aliased:]
    (qd_ref, kd_ref, vd_ref, gd_ref, qs_ref, ks_ref, vs_ref, gs_ref, ga_ref, gb_ref,
     kdb_ref, vdb_ref, ksb_ref, vsb_ref) = outs

    x = x_ref[...]
    ms = jnp.mean(x * x, axis=-1, keepdims=True)
    h = (x * lax.rsqrt(ms + EPS) * nw_ref[...]).astype(BF16)

    def proj(group):
        return _dot(h, w_ref[:, group * WIDTH:(group + 1) * WIDTH])

    lane = lax.broadcasted_iota(jnp.int32, (1, PAIR), 1)
    upper_half = (lane & (HEAD_DIM // 2)) != 0
    cos = cos_ref[...]
    sin = sin_ref[...]

    def rope(s):
        partner = jnp.where(upper_half, pltpu.roll(s, HEAD_DIM // 2, 1),
                            pltpu.roll(s, PAIR - HEAD_DIM // 2, 1))
        return s * cos + partner * sin

    def slab(u, c):
        return u[:, c * PAIR:(c + 1) * PAIR]

    u = proj(G_QD)
    for c in range(N_PAIRS):
        qd_ref[:, c * PAIR:(c + 1) * PAIR] = (rope(slab(u, c)) * QK_SCALE).astype(BF16)
    u = proj(G_KD)
    for c in range(N_PAIRS):
        kd = rope(slab(u, c))
        kd_ref[:, c, :] = kd
        kdb_ref[:, c * PAIR:(c + 1) * PAIR] = kd.astype(BF16)
    u = proj(G_VD)
    for c in range(N_PAIRS):
        vd_ref[:, c, :] = slab(u, c)
    vdb_ref[...] = u.astype(BF16)
    g = proj(G_GD)
    gd_ref[...] = (g * _sigmoid(g)).astype(BF16)
    qs_ref[...] = (proj(G_QS) * QK_SCALE).astype(BF16)
    if feature_major_sb:
        for half, (o_ref, ob_ref) in enumerate(((ks_ref, ksb_ref), (vs_ref, vsb_ref))):
            ut = _nt_dot(wkv_ref[half * WIDTH:(half + 1) * WIDTH, :], h)
            o_ref[...] = ut
            ob_ref[...] = ut.astype(BF16)
    else:
        for group, o_ref, ob_ref in ((G_KS, ks_ref, ksb_ref), (G_VS, vs_ref, vsb_ref)):
            u = proj(group)
            o_ref[...] = u
            ob_ref[...] = u.astype(BF16)
    g = proj(G_GS)
    gs_ref[...] = (g * _sigmoid(g)).astype(BF16)
    for half in range(2):
        ga_ref[:, half * WIDTH:(half + 1) * WIDTH] = _sigmoid(proj(G_GA + half)).astype(BF16)
        gb_ref[:, half * WIDTH:(half + 1) * WIDTH] = _sigmoid(proj(G_GB + half)).astype(BF16)


def _inproj(x, norm_w, w_in_bf, layer, cos_tab, sin_tab, wkv_t_bf=None, kv_stack=None):
    n, d = x.shape
    tm = min(TOKEN_TILE, n)
    seq = cos_tab.shape[0]
    n_tab = seq // tm
    depth, _, cols = w_in_bf.shape
    row = lambda i: (i, 0)
    wide = pl.BlockSpec((tm, WIDTH), row)
    full = pl.BlockSpec((tm, d), row)
    tab = pl.BlockSpec((tm, PAIR), lambda i: (i % n_tab, 0))
    weight = lambda r, c: pl.BlockSpec((None, r, c), lambda i: (layer, 0, 0), pipeline_mode=pl.Buffered(1))
    bf_out = jax.ShapeDtypeStruct((n, WIDTH), BF16)
    bf_full = jax.ShapeDtypeStruct((n, d), BF16)
    in_specs = [full, pl.BlockSpec((None, 1, d), lambda i: (layer, 0, 0)), weight(d, cols), tab, tab]
    args = [x, norm_w, w_in_bf, cos_tab, sin_tab]
    aliases = {}
    if wkv_t_bf is not None:
        in_specs.append(weight(2 * WIDTH, d))
        args.append(wkv_t_bf)
        heads = pl.BlockSpec((None, tm, N_PAIRS, PAIR), lambda i: (layer, i, 0, 0))
        f32_heads = jax.ShapeDtypeStruct((depth, n, N_PAIRS, PAIR), F32)
        sb_spec = pl.BlockSpec((None, None, WIDTH, tm), lambda i: (layer, i // n_tab, 0, i % n_tab))
        sb_f32 = jax.ShapeDtypeStruct((depth, n // seq, WIDTH, seq), F32)
        sb_bf_spec = pl.BlockSpec((None, WIDTH, tm), lambda i: (i // n_tab, 0, i % n_tab))
        sb_bf = jax.ShapeDtypeStruct((n // seq, WIDTH, seq), BF16)
        if kv_stack is not None:
            for out_index, a in zip(KV_OUTPUTS, kv_stack):
                aliases[len(args)] = out_index
                in_specs.append(pl.BlockSpec(memory_space=pl.ANY))
                args.append(a)
    else:
        heads = pl.BlockSpec((tm, N_PAIRS, PAIR), lambda i: (i, 0, 0))
        f32_heads = jax.ShapeDtypeStruct((n, N_PAIRS, PAIR), F32)
        sb_spec = sb_bf_spec = wide
        sb_f32 = jax.ShapeDtypeStruct((n, WIDTH), F32)
        sb_bf = bf_out
    return pl.pallas_call(
        functools.partial(_inproj_kernel, feature_major_sb=wkv_t_bf is not None, n_aliased=len(aliases)),
        grid=(n // tm,),
        in_specs=in_specs,
        out_specs=[wide, heads, heads, wide, wide, sb_spec, sb_spec, wide, full, full,
                   wide, wide, sb_bf_spec, sb_bf_spec],
        out_shape=[bf_out, f32_heads, f32_heads, bf_out, bf_out, sb_f32, sb_f32, bf_out, bf_full, bf_full,
                   bf_out, bf_out, sb_bf, sb_bf],
        input_output_aliases=aliases,
        compiler_params=pltpu.CompilerParams(dimension_semantics=("parallel",),
                                             vmem_limit_bytes=VMEM_LIMIT),
        name="inproj",
    )(*args)


def _post_kernel(x_ref, yda_ref, ysb_ref, ga_ref, gb_ref, wda_ref, wsb_ref, wout_ref, fnw_ref, o_ref,
                 *, final_norm):
    pda = _dot(yda_ref[...], wda_ref[...])
    psb = _dot(ysb_ref[...], wsb_ref[...])
    merged = (ga_ref[...].astype(F32) * pda + gb_ref[...].astype(F32) * psb).astype(BF16)
    out = x_ref[...] + _dot(merged, wout_ref[...])
    if final_norm:
        ms = jnp.mean(out * out, axis=-1, keepdims=True)
        out = out * lax.rsqrt(ms + EPS) * fnw_ref[...]
    o_ref[...] = out


def _post(x, yda, ysb, ga, gb, wda_bf, wsb_bf, wout_bf, final_norm_w, layer, final_norm):
    n, d = x.shape
    tm = min(TOKEN_TILE, n)
    row = lambda i: (i, 0)
    wide = pl.BlockSpec((tm, WIDTH), row)
    full = pl.BlockSpec((tm, d), row)
    weight = lambda r: pl.BlockSpec((None, r, d), lambda i: (layer, 0, 0), pipeline_mode=pl.Buffered(1))
    return pl.pallas_call(
        functools.partial(_post_kernel, final_norm=final_norm),
        grid=(n // tm,),
        in_specs=[full, wide, wide, full, full, weight(WIDTH), weight(WIDTH), weight(d),
                  pl.BlockSpec((1, d), lambda i: (0, 0))],
        out_specs=full,
        out_shape=jax.ShapeDtypeStruct((n, d), F32),
        compiler_params=pltpu.CompilerParams(dimension_semantics=("parallel",),
                                             vmem_limit_bytes=VMEM_LIMIT),
        name="post",
    )(x, yda, ysb, ga, gb, wda_bf, wsb_bf, wout_bf, final_norm_w)


def _split_pair(q):
    lane = lax.broadcasted_iota(jnp.int32, q.shape, 1)
    first = lane < HEAD_DIM
    zero = jnp.zeros_like(q)
    return jnp.concatenate([jnp.where(first, q, zero), jnp.where(first, zero, q)], axis=0)


def _lambda(lam_ref, lam_init):
    lf = lam_ref[...]
    a = jnp.sum(lf[0:1] * lf[1:2], axis=-1, keepdims=True)
    b = jnp.sum(lf[2:3] * lf[3:4], axis=-1, keepdims=True)
    return jnp.exp(a) - jnp.exp(b) + lam_init


def _softmax_step(s, v, carry):
    m, l, acc = carry
    m_new = jnp.maximum(m, jnp.max(s, axis=-1, keepdims=True))
    alpha = jnp.exp(m - m_new)
    p = jnp.exp(s - m_new)
    l = alpha * l + jnp.sum(p, axis=-1, keepdims=True)
    acc = alpha * acc + _dot(p.astype(BF16), v)
    return m_new, l, acc


def _da_finish(carry, lam, sw, gate, lam_init):
    _, l, acc = carry
    t = acc.shape[0] // 2
    o = acc[:t] * (1.0 / l[:t]) - lam * (acc[t:] * (1.0 / l[t:]))
    return _da_gate(o, sw, gate, lam_init)


def _da_gate(o, sw, gate, lam_init):
    ms = jnp.mean(o * o, axis=-1, keepdims=True)
    y = (o * lax.rsqrt(ms + EPS) * sw) * (1.0 - lam_init)
    return (y * gate.astype(F32)).astype(BF16)


def _neg_tri(n):
    j = lax.broadcasted_iota(jnp.int32, (n, n), 0)
    s = lax.broadcasted_iota(jnp.int32, (n, n), 1)
    return jnp.where(j >= s, -1.0, 0.0).astype(BF16)


def _sb_step(z, pv, neg_tri, carry, mask=None):
    run, acc = carry
    hi, lo, total = _sb_front(z, mask)
    c = _dot(hi, neg_tri) + _dot(lo, neg_tri)
    return run - total, acc + pv(_sb_back(z, c, run, mask))


def _sb_front(z, mask=None):
    neg_abs = lax.bitcast_convert_type(lax.bitcast_convert_type(z * LOG2E, jnp.uint32) | SIGN_BIT, F32)
    sp = jnp.maximum(z, 0.0) + jnp.log(1.0 + jnp.exp2(neg_abs))
    if mask is not None:
        sp = jnp.where(mask, sp, 0.0)
    hi = sp.astype(BF16)
    lo = (sp - hi.astype(F32)).astype(BF16)
    return hi, lo, jnp.sum(sp, axis=-1, keepdims=True)


def _sb_back(z, c, run, mask=None):
    if run.shape[1] != 1:
        run = jnp.concatenate([run] * (z.shape[1] // run.shape[1]), axis=1)
    a = jnp.exp2((z + c + run) * LOG2E)
    if mask is not None:
        a = jnp.where(mask, a, 0.0)
    return a.astype(BF16)


def _sb_alive(run):
    return (jnp.max(run) > DEAD_LOG).astype(jnp.int32)


def _sb_finish(carry, gate):
    _, acc = carry
    t = acc.shape[0] // 2
    lane = lax.broadcasted_iota(jnp.int32, (t, PAIR), 1)
    o = jnp.where(lane < HEAD_DIM, acc[:t], acc[t:])
    return (o * gate.astype(F32)).astype(BF16)


def _pair(h):
    return slice(h * PAIR, (h + 1) * PAIR)


def _da_prompt_kernel(lam_ref, sw_ref, q_ref, k_ref, v_ref, g_ref, o_ref, q2_ref, s_ref, mx_ref, acc_ref,
                      *, lam_init):
    i = pl.program_id(1)
    tq = q_ref.shape[0]
    row = lax.broadcasted_iota(jnp.int32, (2 * tq, tq), 0) % tq
    col = lax.broadcasted_iota(jnp.int32, (2 * tq, tq), 1)
    diag_mask = (col // CHUNK) <= (row // CHUNK)
    ones = jnp.ones((tq, PAIR), BF16)

    def scores(h, start, mask=None):
        s = _nt_dot(q2_ref[h], k_ref[pl.ds(start, tq), _pair(h)]) * LOG2E
        if mask is not None:
            s = jnp.where(mask, s, NEG_INF)
        s_ref[h, :, pl.ds(start, tq)] = s
        return jnp.maximum(s[:, :PAIR], s[:, PAIR:])

    for h in range(N_PAIRS):
        q2_ref[h] = _split_pair(q_ref[:, _pair(h)])
    for h in range(N_PAIRS):
        mx_ref[h] = scores(h, pl.multiple_of(i * tq, tq), diag_mask)

    @pl.loop(0, (i + 1) // 2)
    def _(kb2):
        for kb in (2 * kb2, jnp.minimum(2 * kb2 + 1, i - 1)):
            for h in range(N_PAIRS):
                mx_ref[h] = jnp.maximum(mx_ref[h], scores(h, pl.multiple_of(kb * tq, tq)))

    for h in range(N_PAIRS):
        m2 = jnp.max(mx_ref[h], axis=-1, keepdims=True)
        mx_ref[h] = jnp.broadcast_to(m2, mx_ref.shape[1:])
    acc_ref[...] = jnp.zeros(acc_ref.shape, F32)

    def weigh(h, kb):
        start = pl.multiple_of(kb * tq, tq)
        m2 = mx_ref[h]
        p = jnp.exp2(s_ref[h, :, pl.ds(start, tq)] - jnp.concatenate([m2, m2], axis=1))
        v_ones = jnp.concatenate([v_ref[pl.ds(start, tq), _pair(h)], ones], axis=1)
        return _dot(p.astype(BF16), v_ones)

    @pl.loop(0, (i + 1) // 2)
    def _(kb2):
        for h in range(N_PAIRS):
            acc_ref[h] += weigh(h, 2 * kb2) + weigh(h, 2 * kb2 + 1)

    @pl.when(i % 2 == 0)
    def _():
        for h in range(N_PAIRS):
            acc_ref[h] += weigh(h, i)

    lam = _lambda(lam_ref, lam_init)
    sw = sw_ref[...]
    for h in range(N_PAIRS):
        acc = acc_ref[h]
        o = acc[:tq, :PAIR] / acc[:tq, PAIR:] - lam * (acc[tq:, :PAIR] / acc[tq:, PAIR:])
        o_ref[:, _pair(h)] = _da_gate(o, sw, g_ref[:, _pair(h)], lam_init)


def _sb_prompt_kernel(q_ref, kt_ref, vt_ref, g_ref, o_ref, q2_ref, run_ref, acc_ref):
    i = pl.program_id(1)
    tq = q_ref.shape[0]
    row = lax.broadcasted_iota(jnp.int32, (2 * tq, tq), 0) % tq
    col = lax.broadcasted_iota(jnp.int32, (2 * tq, tq), 1)
    diag_mask = col < row
    neg_tri = _neg_tri(tq)

    def block(start, mask=None):
        for h in range(N_PAIRS):
            z = _dot(q2_ref[h], kt_ref[_pair(h), pl.ds(start, tq)])
            run, acc = _sb_step(z, lambda a: _nt_dot(a, vt_ref[_pair(h), pl.ds(start, tq)]), neg_tri,
                                (run_ref[h], acc_ref[h]), mask)
            run_ref[h] = run
            acc_ref[h] = acc

    run_ref[...] = jnp.zeros(run_ref.shape, F32)
    acc_ref[...] = jnp.zeros(acc_ref.shape, F32)
    for h in range(N_PAIRS):
        q2_ref[h] = _split_pair(q_ref[:, _pair(h)])
    block(pl.multiple_of(i * tq, tq), diag_mask)

    def earlier_block(state):
        kb, _ = state
        block(pl.multiple_of(kb * tq, tq))
        return kb - 1, _sb_alive(run_ref[...])

    lax.while_loop(lambda state: (state[0] >= 0) & (state[1] > 0), earlier_block,
                   (i - 1, _sb_alive(run_ref[...])))
    for h in range(N_PAIRS):
        o_ref[:, _pair(h)] = _sb_finish((run_ref[h], acc_ref[h]), g_ref[:, _pair(h)])


def _prompt_attention(kernel_fn, name, batch, seq, q, k, v, g, feature_major_kv, scratch,
                      prefix_args=(), prefix_specs=()):
    tq = ATTN_TILE
    qspec = pl.BlockSpec((None, tq, WIDTH), lambda b, i: (b, i, 0))
    shape3 = lambda a: a.reshape(batch, seq, WIDTH)
    if feature_major_kv:
        kvspec = pl.BlockSpec((None, WIDTH, seq), lambda b, i: (b, 0, 0))
    else:
        kvspec = pl.BlockSpec((None, seq, WIDTH), lambda b, i: (b, 0, 0))
        k, v = shape3(k), shape3(v)
    out = pl.pallas_call(
        kernel_fn,
        grid=(batch, seq // tq),
        in_specs=list(prefix_specs) + [qspec, kvspec, kvspec, qspec],
        out_specs=qspec,
        out_shape=jax.ShapeDtypeStruct((batch, seq, WIDTH), BF16),
        scratch_shapes=scratch,
        compiler_params=pltpu.CompilerParams(dimension_semantics=("parallel", "arbitrary"),
                                             vmem_limit_bytes=VMEM_LIMIT),
        name=name,
    )(*prefix_args, shape3(q), k, v, shape3(g))
    return out.reshape(batch * seq, WIDTH)


def _pad_rows(a, rows):
    return jnp.concatenate([a, jnp.zeros((rows - a.shape[0], a.shape[1]), a.dtype)], axis=0)


def _da_decode_kernel(lam_ref, sw_ref, q_ref, ck_ref, cv_ref, nk_ref, nv_ref, g_ref, o_ref,
                      q_all_ref, m_ref, l_ref, acc_ref, *, lam_init, past_len):
    j = pl.program_id(1)
    t = q_ref.shape[0]
    n_rows = N_PAIRS * 2 * t
    q_head = lax.broadcasted_iota(jnp.int32, (n_rows, 1), 0) // (2 * t)

    @pl.when(j == 0)
    def _():
        m_ref[...] = jnp.full(m_ref.shape, NEG_INF, F32)
        l_ref[...] = jnp.zeros(l_ref.shape, F32)
        acc_ref[...] = jnp.zeros(acc_ref.shape, F32)
        q_all_ref[...] = jnp.concatenate([_split_pair(q_ref[:, _pair(h)]) for h in range(N_PAIRS)], axis=0)

    key_head = lax.broadcasted_iota(jnp.int32, (1, ck_ref.shape[0]), 1) % N_PAIRS
    s = jnp.where(key_head == q_head, _nt_dot(q_all_ref[...], ck_ref[...].astype(BF16)), NEG_INF)
    m, l, acc = _softmax_step(s, cv_ref[...].astype(BF16), (m_ref[...], l_ref[...], acc_ref[...]))
    m_ref[...] = m
    l_ref[...] = l
    acc_ref[...] = acc

    @pl.when(j == pl.num_programs(1) - 1)
    def _():
        lam = _lambda(lam_ref, lam_init)
        sw = sw_ref[...]
        stack = lambda ref: _pad_rows(jnp.concatenate([ref[:, _pair(h)] for h in range(N_PAIRS)], axis=0), PAIR)
        k_idx = lax.broadcasted_iota(jnp.int32, (1, PAIR), 1)
        q_pos = past_len + lax.broadcasted_iota(jnp.int32, (n_rows, 1), 0) % t
        mask = ((k_idx // t == q_head) & (((past_len + k_idx % t) // CHUNK) <= (q_pos // CHUNK)))
        s = jnp.where(mask, _nt_dot(q_all_ref[...], stack(nk_ref)), NEG_INF)
        m, l, acc = _softmax_step(s, stack(nv_ref), (m_ref[...], l_ref[...], acc_ref[...]))
        for h in range(N_PAIRS):
            rows = slice(h * 2 * t, (h + 1) * 2 * t)
            o_ref[:, _pair(h)] = _da_finish((m[rows], l[rows], acc[rows]), lam, sw, g_ref[:, _pair(h)], lam_init)


def _sb_decode_kernel(q_ref, ckt_ref, cvt_ref, g_ref, in_a_ref, in_b_ref, o_ref, run_out_ref, acc_out_ref,
                      run_ref, acc_ref, *, first):
    j = pl.program_id(1)
    t = q_ref.shape[0]
    sub = ATTN_TILE
    n_sub = ckt_ref.shape[1] // sub
    neg_tri = _neg_tri(sub)

    @pl.when(j == 0)
    def _():
        if first:
            q_idx = lax.broadcasted_iota(jnp.int32, (2 * t, PAIR), 0) % t
            k_idx = lax.broadcasted_iota(jnp.int32, (2 * t, PAIR), 1)
            mask = k_idx < q_idx
            for h in range(N_PAIRS):
                q2 = _split_pair(q_ref[:, _pair(h)])
                nv = _pad_rows(in_b_ref[:, _pair(h)], PAIR)
                init = (jnp.zeros((2 * t, 1), F32), jnp.zeros((2 * t, PAIR), F32))
                run, acc = _sb_step(_nt_dot(q2, _pad_rows(in_a_ref[:, _pair(h)], PAIR)), lambda a: _dot(a, nv),
                                    neg_tri[:PAIR, :PAIR], init, mask)
                run_ref[h] = run
                acc_ref[h] = acc
        else:
            run_ref[...] = in_a_ref[...]
            acc_ref[...] = in_b_ref[...]

    for c in reversed(range(n_sub)):
        cols = slice(c * sub, (c + 1) * sub)

        @pl.when(_sb_alive(run_ref[...]) > 0)
        def _():
            for h in range(N_PAIRS):
                q2 = _split_pair(q_ref[:, _pair(h)])
                z = _dot(q2, ckt_ref[_pair(h), cols].astype(BF16))
                run, acc = _sb_step(z, lambda a: _nt_dot(a, cvt_ref[_pair(h), cols].astype(BF16)), neg_tri,
                                    (run_ref[h], acc_ref[h]))
                run_ref[h] = run
                acc_ref[h] = acc

    @pl.when(j == pl.num_programs(1) - 1)
    def _():
        run_out_ref[...] = run_ref[...]
        acc_out_ref[...] = acc_ref[...]
        for h in range(N_PAIRS):
            o_ref[:, _pair(h)] = _sb_finish((run_ref[h], acc_ref[h]), g_ref[:, _pair(h)])


def _sb_decode(layer, t, q, cache_kt, cache_vt, nk, nv, g):
    batch, past = cache_kt.shape[1], cache_kt.shape[3]
    n_blocks = past // CACHE_TILE
    tok = pl.BlockSpec((None, t, WIDTH), lambda b, j: (b, 0, 0))
    run_spec = pl.BlockSpec((None, N_PAIRS, 2 * t, 1), lambda b, j: (b, 0, 0, 0))
    acc_spec = pl.BlockSpec((None, N_PAIRS, 2 * t, PAIR), lambda b, j: (b, 0, 0, 0))
    run_shape = jax.ShapeDtypeStruct((batch, N_PAIRS, 2 * t, 1), F32)
    acc_shape = jax.ShapeDtypeStruct((batch, N_PAIRS, 2 * t, PAIR), F32)
    shape3 = lambda a: a.reshape(batch, t, WIDTH)

    def call(first, steps, newest_block, in_a, in_b, in_spec_a, in_spec_b):
        cache = pl.BlockSpec((None, None, WIDTH, CACHE_TILE), lambda b, j: (layer, b, 0, newest_block - j))
        return pl.pallas_call(
            functools.partial(_sb_decode_kernel, first=first),
            grid=(batch, steps),
            in_specs=[tok, cache, cache, tok, in_spec_a, in_spec_b],
            out_specs=[tok, run_spec, acc_spec],
            out_shape=[jax.ShapeDtypeStruct((batch, t, WIDTH), BF16), run_shape, acc_shape],
            scratch_shapes=[pltpu.VMEM((N_PAIRS, 2 * t, 1), F32), pltpu.VMEM((N_PAIRS, 2 * t, PAIR), F32)],
            compiler_params=pltpu.CompilerParams(dimension_semantics=("parallel", "arbitrary"),
                                                 vmem_limit_bytes=VMEM_LIMIT),
            name="sb_decode",
        )(shape3(q), cache_kt, cache_vt, shape3(g), in_a, in_b)

    y, run, acc = call(True, 1, n_blocks - 1, shape3(nk), shape3(nv), tok, tok)
    if n_blocks > 1:
        y = lax.cond(jnp.max(run) > DEAD_LOG,
                     lambda: call(False, n_blocks - 1, n_blocks - 2, run, acc, run_spec, acc_spec)[0],
                     lambda: y)
    return y.reshape(batch * t, WIDTH)


def _decode_attention(kernel_fn, name, t, q, cache_k, cache_v, cache_spec, nk, nv, g, scratch,
                      prefix_args=(), prefix_specs=()):
    batch = cache_k.shape[1]
    tok = pl.BlockSpec((None, t, WIDTH), lambda b, j: (b, 0, 0))
    shape3 = lambda a: a.reshape(batch, t, WIDTH)
    out = pl.pallas_call(
        kernel_fn,
        grid=(batch, cache_k.size // (cache_k.shape[0] * batch * WIDTH * CACHE_TILE)),
        in_specs=list(prefix_specs) + [tok, cache_spec, cache_spec, tok, tok, tok],
        out_specs=tok,
        out_shape=jax.ShapeDtypeStruct((batch, t, WIDTH), BF16),
        scratch_shapes=scratch,
        compiler_params=pltpu.CompilerParams(dimension_semantics=("parallel", "arbitrary"),
                                             vmem_limit_bytes=VMEM_LIMIT),
        name=name,
    )(*prefix_args, shape3(q), cache_k, cache_v, shape3(nk), shape3(nv), shape3(g))
    return out.reshape(batch * t, WIDTH)


def _rope_tables(pos):
    half = HEAD_DIM // 2
    inv = ROPE_THETA ** (-jnp.arange(half, dtype=F32) / half)
    ang = pos.astype(F32)[:, None] * inv[None, :]
    cos = jnp.cos(ang)
    sin = jnp.sin(ang)
    reps = PAIR // HEAD_DIM
    return (jnp.tile(jnp.concatenate([cos, cos], axis=1), (1, reps)),
            jnp.tile(jnp.concatenate([-sin, sin], axis=1), (1, reps)))


def kernel(x_prompt, x_sample, cache_da_k, cache_da_v, cache_sb_k, cache_sb_v, norm_w, w_in, lambda_qk,
           subln_w, w_proj_da, w_proj_sb, w_out, final_norm_w):
    batch, seq, d = x_prompt.shape
    dec_batch, dec_seq, _ = x_sample.shape
    depth, _, past_len, da_heads, _ = cache_da_k.shape
    sb_heads = cache_sb_k.shape[3]
    assert seq % ATTN_TILE == 0 and ATTN_TILE % CHUNK == 0 and seq % TOKEN_TILE == 0
    assert dec_batch * dec_seq <= TOKEN_TILE and past_len % CACHE_TILE == 0 and N_PAIRS * dec_seq <= PAIR
    assert da_heads == N_PAIRS and cache_da_k.shape[4] == PAIR and sb_heads * HEAD_DIM == WIDTH

    cos_p, sin_p = _rope_tables(jnp.arange(seq, dtype=jnp.int32))
    cos_s, sin_s = _rope_tables(jnp.tile(past_len + jnp.arange(dec_seq, dtype=jnp.int32), dec_batch))

    w_in_bf = w_in.astype(BF16)
    wkv_t_bf = jnp.swapaxes(w_in[:, :, G_KS * WIDTH:(G_VS + 1) * WIDTH], 1, 2).astype(BF16)
    wda_bf = w_proj_da.astype(BF16)
    wsb_bf = w_proj_sb.astype(BF16)
    wout_bf = w_out.astype(BF16)
    norm_w3 = norm_w.reshape(depth, 1, d)
    subln_w3 = subln_w.reshape(depth, 1, PAIR)
    fnw = final_norm_w.reshape(1, d)
    feature_major = lambda c: jnp.transpose(c, (0, 1, 3, 4, 2)).reshape(depth, dec_batch, WIDTH, past_len)
    cskt, csvt = feature_major(cache_sb_k), feature_major(cache_sb_v)
    head_rows = lambda c: c.reshape(depth, dec_batch, past_len * da_heads, PAIR)
    cdk, cdv = head_rows(cache_da_k), head_rows(cache_da_v)

    xp = x_prompt.reshape(batch * seq, d)
    xs = x_sample.reshape(dec_batch * dec_seq, d)
    kv_stack, rows_s = None, []
    for layer in range(depth):
        lam_init = 0.8 - 0.6 * math.exp(-0.3 * layer)
        lam_args = (lambda_qk, subln_w3)
        lam_specs = (pl.BlockSpec((None,) + lambda_qk.shape[1:], lambda b, i, layer=layer: (layer, 0, 0)),
                     pl.BlockSpec((None, 1, PAIR), lambda b, i, layer=layer: (layer, 0, 0)))
        last = layer == depth - 1

        (qd, kd, vd, gd, qs, kst, vst, gs, ga, gb, kdb, vdb, kstb, vstb) = _inproj(
            xp, norm_w3, w_in_bf, layer, cos_p, sin_p, wkv_t_bf, kv_stack)
        kv_stack = (kd, vd, kst, vst)
        rows2 = (N_PAIRS, 2 * ATTN_TILE)
        yda = _prompt_attention(
            functools.partial(_da_prompt_kernel, lam_init=lam_init), "da_prompt",
            batch, seq, qd, kdb, vdb, gd, False,
            [pltpu.VMEM(rows2 + (PAIR,), BF16), pltpu.VMEM(rows2 + (seq,), F32),
             pltpu.VMEM(rows2 + (PAIR,), F32), pltpu.VMEM(rows2 + (2 * PAIR,), F32)],
            lam_args, lam_specs)
        ysb = _prompt_attention(
            _sb_prompt_kernel, "sb_prompt", batch, seq, qs, kstb, vstb, gs, True,
            [pltpu.VMEM(rows2 + (PAIR,), BF16), pltpu.VMEM(rows2 + (1,), F32),
             pltpu.VMEM(rows2 + (PAIR,), F32)])
        xp = _post(xp, yda, ysb, ga, gb, wda_bf, wsb_bf, wout_bf, fnw, layer, last)

        (qd, kd, vd, gd, qs, ks, vs, gs, ga, gb, kdb, vdb, ksb, vsb) = _inproj(
            xs, norm_w3, w_in_bf, layer, cos_s, sin_s)
        all_rows = N_PAIRS * 2 * dec_seq
        yda = _decode_attention(
            functools.partial(_da_decode_kernel, lam_init=lam_init, past_len=past_len), "da_decode",
            dec_seq, qd, cdk, cdv,
            pl.BlockSpec((None, None, CACHE_TILE * da_heads, PAIR), lambda b, j, layer=layer: (layer, b, j, 0)),
            kdb, vdb, gd,
            [pltpu.VMEM((all_rows, PAIR), BF16), pltpu.VMEM((all_rows, 1), F32),
             pltpu.VMEM((all_rows, 1), F32), pltpu.VMEM((all_rows, PAIR), F32)],
            lam_args, lam_specs)
        ysb = _sb_decode(layer, dec_seq, qs, cskt, csvt, ksb, vsb, gs)
        xs = _post(xs, yda, ysb, ga, gb, wda_bf, wsb_bf, wout_bf, fnw, layer, last)
        rows_s.append((kd, vd, ks, vs))

    def stack(rows, idx, shape):
        return jnp.stack([r[idx] for r in rows], axis=0).reshape((depth,) + shape)

    def token_major(a):
        return jnp.transpose(a.reshape(depth, batch, sb_heads, HEAD_DIM, seq), (0, 1, 4, 2, 3))

    sb_s = (dec_batch, dec_seq, sb_heads, HEAD_DIM)
    da_s = (dec_batch, dec_seq, da_heads, PAIR)
    kd, vd, kst, vst = kv_stack
    return (xp.reshape(batch, seq, d), xs.reshape(dec_batch, dec_seq, d),
            kd.reshape(depth, batch, seq, da_heads, PAIR), vd.reshape(depth, batch, seq, da_heads, PAIR),
            token_major(kst), token_major(vst),
            stack(rows_s, 0, da_s), stack(rows_s, 1, da_s), stack(rows_s, 2, sb_s), stack(rows_s, 3, sb_s))
```

```python
import functools
import math

import jax
import jax.numpy as jnp
import numpy as np
from jax import lax
from jax.experimental import pallas as pl
from jax.experimental.pallas import tpu as pltpu

F32 = jnp.float32
BF16 = jnp.bfloat16

CHUNK = 64
HEAD_DIM = 64
PAIR = 2 * HEAD_DIM
N_PAIRS = 4
WIDTH = N_PAIRS * PAIR
ROPE_THETA = 10000.0
EPS = 1e-6
NEG_INF = -1e30
QK_SCALE = HEAD_DIM ** -0.5
LOG2E = math.log2(math.e)
SIGN_BIT = np.uint32(0x80000000)
DEAD_LOG = -150.0 * math.log(2.0)

TOKEN_TILE = 512
ATTN_TILE = 256
CACHE_TILE = 1024
VMEM_LIMIT = 56 * 1024 * 1024

G_QD, G_KD, G_VD, G_GD, G_QS, G_KS, G_VS, G_GS, G_GA = 0, 1, 2, 3, 4, 5, 6, 7, 8
KV_OUTPUTS = (1, 2, 5, 6)


def _sigmoid(x):
    return 1.0 / (1.0 + jnp.exp(-x))


def _nt_dot(a, b):
    return lax.dot_general(a, b, (((1,), (1,)), ((), ())), preferred_element_type=F32)


def _dot(a, b):
    return jnp.dot(a, b, preferred_element_type=F32)


def _inproj_kernel(*refs, feature_major_sb, n_aliased, zero_slots_except):
    if feature_major_sb:
        x_ref, nw_ref, w_ref, cos_ref, sin_ref, wkv_ref = refs[:6]
        outs = refs[6 + n_aliased:]
    else:
        x_ref, nw_ref, w_ref, cos_ref, sin_ref = refs[:5]
        outs = refs[5 + n_aliased:]
    qd_ref, kd_ref, vd_ref, gd_ref, qs_ref, ks_ref, vs_ref, gs_ref, kdb_ref, vdb_ref, ksb_ref, vsb_ref = outs
    if zero_slots_except is not None:
        stacked = (kd_ref, vd_ref, ks_ref, vs_ref)
        for ref in stacked:
            for slot in range(ref.shape[0]):
                if slot != zero_slots_except:
                    ref[slot] = jnp.zeros(ref.shape[1:], ref.dtype)
        kd_ref, vd_ref, ks_ref, vs_ref = (ref.at[zero_slots_except] for ref in stacked)

    x = x_ref[...]
    ms = jnp.mean(x * x, axis=-1, keepdims=True)
    h = (x * lax.rsqrt(ms + EPS) * nw_ref[...]).astype(BF16)

    def proj(group):
        return _dot(h, w_ref[:, group * WIDTH:(group + 1) * WIDTH])

    lane = lax.broadcasted_iota(jnp.int32, (1, PAIR), 1)
    upper_half = (lane & (HEAD_DIM // 2)) != 0
    cos = cos_ref[...]
    sin = sin_ref[...]

    def rope(s):
        partner = jnp.where(upper_half, pltpu.roll(s, HEAD_DIM // 2, 1),
                            pltpu.roll(s, PAIR - HEAD_DIM // 2, 1))
        return s * cos + partner * sin

    def slab(u, c):
        return u[:, c * PAIR:(c + 1) * PAIR]

    u = proj(G_QD)
    for c in range(N_PAIRS):
        qd_ref[:, c * PAIR:(c + 1) * PAIR] = (rope(slab(u, c)) * QK_SCALE).astype(BF16)
    u = proj(G_KD)
    for c in range(N_PAIRS):
        kd = rope(slab(u, c))
        kd_ref[:, c, :] = kd
        kdb_ref[:, c * PAIR:(c + 1) * PAIR] = kd.astype(BF16)
    u = proj(G_VD)
    for c in range(N_PAIRS):
        vd_ref[:, c, :] = slab(u, c)
    vdb_ref[...] = u.astype(BF16)
    g = proj(G_GD)
    gd_ref[...] = (g * _sigmoid(g)).astype(BF16)
    qs_ref[...] = (proj(G_QS) * QK_SCALE).astype(BF16)
    if feature_major_sb:
        for half, (o_ref, ob_ref) in enumerate(((ks_ref, ksb_ref), (vs_ref, vsb_ref))):
            ut = _nt_dot(wkv_ref[half * WIDTH:(half + 1) * WIDTH, :], h)
            o_ref[...] = ut
            ob_ref[...] = ut.astype(BF16)
    else:
        for group, o_ref, ob_ref in ((G_KS, ks_ref, ksb_ref), (G_VS, vs_ref, vsb_ref)):
            u = proj(group)
            o_ref[...] = u
            ob_ref[...] = u.astype(BF16)
    g = proj(G_GS)
    gs_ref[...] = (g * _sigmoid(g)).astype(BF16)


def _inproj(x, norm_w, w_in_bf, layer, cos_tab, sin_tab, wkv_t_bf=None, kv_stack=None):
    n, d = x.shape
    creates_stack = wkv_t_bf is not None and kv_stack is None
    tm = min(TOKEN_TILE // 2 if creates_stack else TOKEN_TILE, n)
    seq = cos_tab.shape[0]
    n_tab = seq // tm
    depth, _, cols = w_in_bf.shape
    row = lambda i: (i, 0)
    wide = pl.BlockSpec((tm, WIDTH), row)
    full = pl.BlockSpec((tm, d), row)
    tab = pl.BlockSpec((tm, PAIR), lambda i: (i % n_tab, 0))
    weight = lambda r, c: pl.BlockSpec((None, r, c), lambda i: (layer, 0, 0), pipeline_mode=pl.Buffered(1))
    bf_out = jax.ShapeDtypeStruct((n, WIDTH), BF16)
    in_specs = [full, pl.BlockSpec((None, 1, d), lambda i: (layer, 0, 0)), weight(d, G_GA * WIDTH), tab, tab]
    args = [x, norm_w, w_in_bf, cos_tab, sin_tab]
    aliases = {}
    if wkv_t_bf is not None:
        in_specs.append(weight(2 * WIDTH, d))
        args.append(wkv_t_bf)
        slots, slot = (depth, 0) if creates_stack else (None, layer)
        heads = pl.BlockSpec((slots, tm, N_PAIRS, PAIR), lambda i: (slot, i, 0, 0))
        f32_heads = jax.ShapeDtypeStruct((depth, n, N_PAIRS, PAIR), F32)
        sb_spec = pl.BlockSpec((slots, None, WIDTH, tm), lambda i: (slot, i // n_tab, 0, i % n_tab))
        sb_f32 = jax.ShapeDtypeStruct((depth, n // seq, WIDTH, seq), F32)
        sb_bf_spec = pl.BlockSpec((None, WIDTH, tm), lambda i: (i // n_tab, 0, i % n_tab))
        sb_bf = jax.ShapeDtypeStruct((n // seq, WIDTH, seq), BF16)
        if kv_stack is not None:
            for out_index, a in zip(KV_OUTPUTS, kv_stack):
                aliases[len(args)] = out_index
                in_specs.append(pl.BlockSpec(memory_space=pl.ANY))
                args.append(a)
    else:
        heads = pl.BlockSpec((tm, N_PAIRS, PAIR), lambda i: (i, 0, 0))
        f32_heads = jax.ShapeDtypeStruct((n, N_PAIRS, PAIR), F32)
        sb_spec = sb_bf_spec = wide
        sb_f32 = jax.ShapeDtypeStruct((n, WIDTH), F32)
        sb_bf = bf_out
    return pl.pallas_call(
        functools.partial(_inproj_kernel, feature_major_sb=wkv_t_bf is not None, n_aliased=len(aliases),
                          zero_slots_except=layer if creates_stack else None),
        grid=(n // tm,),
        in_specs=in_specs,
        out_specs=[wide, heads, heads, wide, wide, sb_spec, sb_spec, wide, wide, wide, sb_bf_spec, sb_bf_spec],
        out_shape=[bf_out, f32_heads, f32_heads, bf_out, bf_out, sb_f32, sb_f32, bf_out,
                   bf_out, bf_out, sb_bf, sb_bf],
        input_output_aliases=aliases,
        compiler_params=pltpu.CompilerParams(dimension_semantics=("parallel",),
                                             vmem_limit_bytes=VMEM_LIMIT),
        name="inproj",
    )(*args)


def _post_kernel(x_ref, nw_ref, wg_ref, yda_ref, ysb_ref, wda_ref, wsb_ref, wout_ref, fnw_ref, o_ref,
                 *, final_norm):
    x = x_ref[...]
    d = x.shape[1]
    ms = jnp.mean(x * x, axis=-1, keepdims=True)
    h = (x * lax.rsqrt(ms + EPS) * nw_ref[...]).astype(BF16)
    merged = (_sigmoid(_dot(h, wg_ref[:, :d])) * _dot(yda_ref[...], wda_ref[...])
              + _sigmoid(_dot(h, wg_ref[:, d:])) * _dot(ysb_ref[...], wsb_ref[...])).astype(BF16)
    out = x + _dot(merged, wout_ref[...])
    if final_norm:
        ms = jnp.mean(out * out, axis=-1, keepdims=True)
        out = out * lax.rsqrt(ms + EPS) * fnw_ref[...]
    o_ref[...] = out


def _post(x, norm_w, w_in_bf, yda, ysb, wda_bf, wsb_bf, wout_bf, final_norm_w, layer, final_norm):
    n, d = x.shape
    tm = min(TOKEN_TILE, n)
    assert G_GA * WIDTH % (2 * d) == 0 and (G_GA * WIDTH + 2 * d) == w_in_bf.shape[2]
    row = lambda i: (i, 0)
    wide = pl.BlockSpec((tm, WIDTH), row)
    full = pl.BlockSpec((tm, d), row)
    weight = lambda r: pl.BlockSpec((None, r, d), lambda i: (layer, 0, 0), pipeline_mode=pl.Buffered(1))
    gate_weight = pl.BlockSpec((None, d, 2 * d), lambda i: (layer, 0, G_GA * WIDTH // (2 * d)),
                               pipeline_mode=pl.Buffered(1))
    return pl.pallas_call(
        functools.partial(_post_kernel, final_norm=final_norm),
        grid=(n // tm,),
        in_specs=[full, pl.BlockSpec((None, 1, d), lambda i: (layer, 0, 0)), gate_weight, wide, wide,
                  weight(WIDTH), weight(WIDTH), weight(d), pl.BlockSpec((1, d), lambda i: (0, 0))],
        out_specs=full,
        out_shape=jax.ShapeDtypeStruct((n, d), F32),
        compiler_params=pltpu.CompilerParams(dimension_semantics=("parallel",),
                                             vmem_limit_bytes=VMEM_LIMIT),
        name="post",
    )(x, norm_w, w_in_bf, yda, ysb, wda_bf, wsb_bf, wout_bf, final_norm_w)


def _split_pair(q):
    lane = lax.broadcasted_iota(jnp.int32, q.shape, 1)
    first = lane < HEAD_DIM
    zero = jnp.zeros_like(q)
    return jnp.concatenate([jnp.where(first, q, zero), jnp.where(first, zero, q)], axis=0)


def _lambda(lam_ref, lam_init):
    lf = lam_ref[...]
    a = jnp.sum(lf[0:1] * lf[1:2], axis=-1, keepdims=True)
    b = jnp.sum(lf[2:3] * lf[3:4], axis=-1, keepdims=True)
    return jnp.exp(a) - jnp.exp(b) + lam_init


def _softmax_step(s, v, carry):
    m, l, acc = carry
    m_new = jnp.maximum(m, jnp.max(s, axis=-1, keepdims=True))
    alpha = jnp.exp(m - m_new)
    p = jnp.exp(s - m_new)
    l = alpha * l + jnp.sum(p, axis=-1, keepdims=True)
    acc = alpha * acc + _dot(p.astype(BF16), v)
    return m_new, l, acc


def _da_finish(carry, lam, sw, gate, lam_init):
    _, l, acc = carry
    t = acc.shape[0] // 2
    o = acc[:t] * (1.0 / l[:t]) - lam * (acc[t:] * (1.0 / l[t:]))
    return _da_gate(o, sw, gate, lam_init)


def _da_gate(o, sw, gate, lam_init):
    ms = jnp.mean(o * o, axis=-1, keepdims=True)
    y = (o * lax.rsqrt(ms + EPS) * sw) * (1.0 - lam_init)
    return (y * gate.astype(F32)).astype(BF16)


def _neg_tri(n):
    j = lax.broadcasted_iota(jnp.int32, (n, n), 0)
    s = lax.broadcasted_iota(jnp.int32, (n, n), 1)
    return jnp.where(j >= s, -1.0, 0.0).astype(BF16)


def _sb_step(z, pv, neg_tri, carry, mask=None):
    run, acc = carry
    hi, lo, total = _sb_front(z, mask)
    c = _dot(hi, neg_tri) + _dot(lo, neg_tri)
    return run - total, acc + pv(_sb_back(z, c, run, mask))


def _sb_front(z, mask=None):
    neg_abs = lax.bitcast_convert_type(lax.bitcast_convert_type(z * LOG2E, jnp.uint32) | SIGN_BIT, F32)
    sp = jnp.maximum(z, 0.0) + jnp.log(1.0 + jnp.exp2(neg_abs))
    if mask is not None:
        sp = jnp.where(mask, sp, 0.0)
    hi = sp.astype(BF16)
    lo = (sp - hi.astype(F32)).astype(BF16)
    return hi, lo, jnp.sum(sp, axis=-1, keepdims=True)


def _sb_back(z, c, run, mask=None):
    if run.shape[1] != 1:
        run = jnp.concatenate([run] * (z.shape[1] // run.shape[1]), axis=1)
    a = jnp.exp2((z + c + run) * LOG2E)
    if mask is not None:
        a = jnp.where(mask, a, 0.0)
    return a.astype(BF16)


def _sb_alive(run):
    return (jnp.max(run) > DEAD_LOG).astype(jnp.int32)


def _sb_finish(carry, gate):
    _, acc = carry
    t = acc.shape[0] // 2
    lane = lax.broadcasted_iota(jnp.int32, (t, PAIR), 1)
    o = jnp.where(lane < HEAD_DIM, acc[:t], acc[t:])
    return (o * gate.astype(F32)).astype(BF16)


def _pair(h):
    return slice(h * PAIR, (h + 1) * PAIR)


def _da_prompt_kernel(lam_ref, sw_ref, q_ref, k_ref, v_ref, g_ref, o_ref, q2_ref, s_ref, mx_ref, acc_ref,
                      *, lam_init):
    i = pl.program_id(1)
    tq = q_ref.shape[0]
    row = lax.broadcasted_iota(jnp.int32, (2 * tq, tq), 0) % tq
    col = lax.broadcasted_iota(jnp.int32, (2 * tq, tq), 1)
    diag_mask = (col // CHUNK) <= (row // CHUNK)
    ones = jnp.ones((tq, PAIR), BF16)

    def scores(h, start, mask=None):
        s = _nt_dot(q2_ref[h], k_ref[pl.ds(start, tq), _pair(h)]) * LOG2E
        if mask is not None:
            s = jnp.where(mask, s, NEG_INF)
        s_ref[h, :, pl.ds(start, tq)] = s
        return jnp.maximum(s[:, :PAIR], s[:, PAIR:])

    for h in range(N_PAIRS):
        q2_ref[h] = _split_pair(q_ref[:, _pair(h)])
    for h in range(N_PAIRS):
        mx_ref[h] = scores(h, pl.multiple_of(i * tq, tq), diag_mask)

    @pl.loop(0, (i + 1) // 2)
    def _(kb2):
        for kb in (2 * kb2, jnp.minimum(2 * kb2 + 1, i - 1)):
            for h in range(N_PAIRS):
                mx_ref[h] = jnp.maximum(mx_ref[h], scores(h, pl.multiple_of(kb * tq, tq)))

    for h in range(N_PAIRS):
        m2 = jnp.max(mx_ref[h], axis=-1, keepdims=True)
        mx_ref[h] = jnp.broadcast_to(m2, mx_ref.shape[1:])

    def weigh(h, kb):
        start = pl.multiple_of(kb * tq, tq)
        m2 = mx_ref[h]
        p = jnp.exp2(s_ref[h, :, pl.ds(start, tq)] - jnp.concatenate([m2, m2], axis=1))
        v_ones = jnp.concatenate([v_ref[pl.ds(start, tq), _pair(h)], ones], axis=1)
        return _dot(p.astype(BF16), v_ones)

    for h in range(N_PAIRS):
        acc_ref[h] = weigh(h, i)

    @pl.loop(0, i // 2)
    def _(kb2):
        for h in range(N_PAIRS):
            acc_ref[h] += weigh(h, 2 * kb2) + weigh(h, 2 * kb2 + 1)

    @pl.when(i % 2 == 1)
    def _():
        for h in range(N_PAIRS):
            acc_ref[h] += weigh(h, i - 1)

    lam = _lambda(lam_ref, lam_init)
    sw = sw_ref[...]
    for h in range(N_PAIRS):
        acc = acc_ref[h]
        o = acc[:tq, :PAIR] / acc[:tq, PAIR:] - lam * (acc[tq:, :PAIR] / acc[tq:, PAIR:])
        o_ref[:, _pair(h)] = _da_gate(o, sw, g_ref[:, _pair(h)], lam_init)


def _sb_prompt_kernel(q_ref, kt_ref, vt_ref, g_ref, o_ref, q2_ref, run_ref, acc_ref):
    i = pl.program_id(1)
    tq = q_ref.shape[0]
    row = lax.broadcasted_iota(jnp.int32, (2 * tq, tq), 0) % tq
    col = lax.broadcasted_iota(jnp.int32, (2 * tq, tq), 1)
    diag_mask = col < row
    neg_tri = _neg_tri(tq)

    def block(start, mask=None):
        for h in range(N_PAIRS):
            z = _dot(q2_ref[h], kt_ref[_pair(h), pl.ds(start, tq)])
            run, acc = _sb_step(z, lambda a: _nt_dot(a, vt_ref[_pair(h), pl.ds(start, tq)]), neg_tri,
                                (run_ref[h], acc_ref[h]), mask)
            run_ref[h] = run
            acc_ref[h] = acc

    run_ref[...] = jnp.zeros(run_ref.shape, F32)
    acc_ref[...] = jnp.zeros(acc_ref.shape, F32)
    for h in range(N_PAIRS):
        q2_ref[h] = _split_pair(q_ref[:, _pair(h)])
    block(pl.multiple_of(i * tq, tq), diag_mask)

    def earlier_block(state):
        kb, _ = state
        block(pl.multiple_of(kb * tq, tq))
        return kb - 1, _sb_alive(run_ref[...])

    lax.while_loop(lambda state: (state[0] >= 0) & (state[1] > 0), earlier_block,
                   (i - 1, _sb_alive(run_ref[...])))
    for h in range(N_PAIRS):
        o_ref[:, _pair(h)] = _sb_finish((run_ref[h], acc_ref[h]), g_ref[:, _pair(h)])


def _prompt_attention(kernel_fn, name, batch, seq, q, k, v, g, feature_major_kv, scratch,
                      prefix_args=(), prefix_specs=()):
    tq = ATTN_TILE
    qspec = pl.BlockSpec((None, tq, WIDTH), lambda b, i: (b, i, 0))
    shape3 = lambda a: a.reshape(batch, seq, WIDTH)
    if feature_major_kv:
        kvspec = pl.BlockSpec((None, WIDTH, seq), lambda b, i: (b, 0, 0))
    else:
        kvspec = pl.BlockSpec((None, seq, WIDTH), lambda b, i: (b, 0, 0))
        k, v = shape3(k), shape3(v)
    out = pl.pallas_call(
        kernel_fn,
        grid=(batch, seq // tq),
        in_specs=list(prefix_specs) + [qspec, kvspec, kvspec, qspec],
        out_specs=qspec,
        out_shape=jax.ShapeDtypeStruct((batch, seq, WIDTH), BF16),
        scratch_shapes=scratch,
        compiler_params=pltpu.CompilerParams(dimension_semantics=("parallel", "arbitrary"),
                                             vmem_limit_bytes=VMEM_LIMIT),
        name=name,
    )(*prefix_args, shape3(q), k, v, shape3(g))
    return out.reshape(batch * seq, WIDTH)


def _pad_rows(a, rows):
    return jnp.concatenate([a, jnp.zeros((rows - a.shape[0], a.shape[1]), a.dtype)], axis=0)


def _da_decode_kernel(lam_ref, sw_ref, q_ref, ck_ref, cv_ref, nk_ref, nv_ref, g_ref, o_ref,
                      q_all_ref, m_ref, l_ref, acc_ref, *, lam_init, past_len):
    j = pl.program_id(1)
    t = q_ref.shape[0]
    n_rows = N_PAIRS * 2 * t
    q_head = lax.broadcasted_iota(jnp.int32, (n_rows, 1), 0) // (2 * t)

    @pl.when(j == 0)
    def _():
        m_ref[...] = jnp.full(m_ref.shape, NEG_INF, F32)
        l_ref[...] = jnp.zeros(l_ref.shape, F32)
        acc_ref[...] = jnp.zeros(acc_ref.shape, F32)
        q_all_ref[...] = jnp.concatenate([_split_pair(q_ref[:, _pair(h)]) for h in range(N_PAIRS)], axis=0)

    key_head = lax.broadcasted_iota(jnp.int32, (1, ck_ref.shape[0]), 1) % N_PAIRS
    s = jnp.where(key_head == q_head, _nt_dot(q_all_ref[...], ck_ref[...].astype(BF16)), NEG_INF)
    m, l, acc = _softmax_step(s, cv_ref[...].astype(BF16), (m_ref[...], l_ref[...], acc_ref[...]))
    m_ref[...] = m
    l_ref[...] = l
    acc_ref[...] = acc

    @pl.when(j == pl.num_programs(1) - 1)
    def _():
        lam = _lambda(lam_ref, lam_init)
        sw = sw_ref[...]
        stack = lambda ref: _pad_rows(jnp.concatenate([ref[:, _pair(h)] for h in range(N_PAIRS)], axis=0), PAIR)
        k_idx = lax.broadcasted_iota(jnp.int32, (1, PAIR), 1)
        q_pos = past_len + lax.broadcasted_iota(jnp.int32, (n_rows, 1), 0) % t
        mask = ((k_idx // t == q_head) & (((past_len + k_idx % t) // CHUNK) <= (q_pos // CHUNK)))
        s = jnp.where(mask, _nt_dot(q_all_ref[...], stack(nk_ref)), NEG_INF)
        m, l, acc = _softmax_step(s, stack(nv_ref), (m_ref[...], l_ref[...], acc_ref[...]))
        for h in range(N_PAIRS):
            rows = slice(h * 2 * t, (h + 1) * 2 * t)
            o_ref[:, _pair(h)] = _da_finish((m[rows], l[rows], acc[rows]), lam, sw, g_ref[:, _pair(h)], lam_init)


def _sb_decode_kernel(q_ref, ckt_ref, cvt_ref, g_ref, in_a_ref, in_b_ref, o_ref, run_out_ref, acc_out_ref,
                      run_ref, acc_ref, *, first):
    j = pl.program_id(1)
    t = q_ref.shape[0]
    sub = ATTN_TILE
    n_sub = ckt_ref.shape[1] // sub
    neg_tri = _neg_tri(sub)

    @pl.when(j == 0)
    def _():
        if first:
            q_idx = lax.broadcasted_iota(jnp.int32, (2 * t, PAIR), 0) % t
            k_idx = lax.broadcasted_iota(jnp.int32, (2 * t, PAIR), 1)
            mask = k_idx < q_idx
            for h in range(N_PAIRS):
                q2 = _split_pair(q_ref[:, _pair(h)])
                nv = _pad_rows(in_b_ref[:, _pair(h)], PAIR)
                init = (jnp.zeros((2 * t, 1), F32), jnp.zeros((2 * t, PAIR), F32))
                run, acc = _sb_step(_nt_dot(q2, _pad_rows(in_a_ref[:, _pair(h)], PAIR)), lambda a: _dot(a, nv),
                                    neg_tri[:PAIR, :PAIR], init, mask)
                run_ref[h] = run
                acc_ref[h] = acc
        else:
            run_ref[...] = in_a_ref[...]
            acc_ref[...] = in_b_ref[...]

    for c in reversed(range(n_sub)):
        cols = slice(c * sub, (c + 1) * sub)

        @pl.when(_sb_alive(run_ref[...]) > 0)
        def _():
            for h in range(N_PAIRS):
                q2 = _split_pair(q_ref[:, _pair(h)])
                z = _dot(q2, ckt_ref[_pair(h), cols].astype(BF16))
                run, acc = _sb_step(z, lambda a: _nt_dot(a, cvt_ref[_pair(h), cols].astype(BF16)), neg_tri,
                                    (run_ref[h], acc_ref[h]))
                run_ref[h] = run
                acc_ref[h] = acc

    @pl.when(j == pl.num_programs(1) - 1)
    def _():
        run_out_ref[...] = run_ref[...]
        acc_out_ref[...] = acc_ref[...]
        for h in range(N_PAIRS):
            o_ref[:, _pair(h)] = _sb_finish((run_ref[h], acc_ref[h]), g_ref[:, _pair(h)])


def _sb_decode(layer, t, q, cache_kt, cache_vt, nk, nv, g):
    batch, past = cache_kt.shape[1], cache_kt.shape[3]
    n_blocks = past // CACHE_TILE
    tok = pl.BlockSpec((None, t, WIDTH), lambda b, j: (b, 0, 0))
    run_spec = pl.BlockSpec((None, N_PAIRS, 2 * t, 1), lambda b, j: (b, 0, 0, 0))
    acc_spec = pl.BlockSpec((None, N_PAIRS, 2 * t, PAIR), lambda b, j: (b, 0, 0, 0))
    run_shape = jax.ShapeDtypeStruct((batch, N_PAIRS, 2 * t, 1), F32)
    acc_shape = jax.ShapeDtypeStruct((batch, N_PAIRS, 2 * t, PAIR), F32)
    shape3 = lambda a: a.reshape(batch, t, WIDTH)

    def call(first, steps, newest_block, in_a, in_b, in_spec_a, in_spec_b):
        cache = pl.BlockSpec((None, None, WIDTH, CACHE_TILE), lambda b, j: (layer, b, 0, newest_block - j))
        return pl.pallas_call(
            functools.partial(_sb_decode_kernel, first=first),
            grid=(batch, steps),
            in_specs=[tok, cache, cache, tok, in_spec_a, in_spec_b],
            out_specs=[tok, run_spec, acc_spec],
            out_shape=[jax.ShapeDtypeStruct((batch, t, WIDTH), BF16), run_shape, acc_shape],
            scratch_shapes=[pltpu.VMEM((N_PAIRS, 2 * t, 1), F32), pltpu.VMEM((N_PAIRS, 2 * t, PAIR), F32)],
            compiler_params=pltpu.CompilerParams(dimension_semantics=("parallel", "arbitrary"),
                                                 vmem_limit_bytes=VMEM_LIMIT),
            name="sb_decode",
        )(shape3(q), cache_kt, cache_vt, shape3(g), in_a, in_b)

    y, run, acc = call(True, 1, n_blocks - 1, shape3(nk), shape3(nv), tok, tok)
    if n_blocks > 1:
        y = lax.cond(jnp.max(run) > DEAD_LOG,
                     lambda: call(False, n_blocks - 1, n_blocks - 2, run, acc, run_spec, acc_spec)[0],
                     lambda: y)
    return y.reshape(batch * t, WIDTH)


def _decode_attention(kernel_fn, name, t, q, cache_k, cache_v, cache_spec, nk, nv, g, scratch,
                      prefix_args=(), prefix_specs=()):
    batch = cache_k.shape[1]
    tok = pl.BlockSpec((None, t, WIDTH), lambda b, j: (b, 0, 0))
    shape3 = lambda a: a.reshape(batch, t, WIDTH)
    out = pl.pallas_call(
        kernel_fn,
        grid=(batch, cache_k.size // (cache_k.shape[0] * batch * WIDTH * CACHE_TILE)),
        in_specs=list(prefix_specs) + [tok, cache_spec, cache_spec, tok, tok, tok],
        out_specs=tok,
        out_shape=jax.ShapeDtypeStruct((batch, t, WIDTH), BF16),
        scratch_shapes=scratch,
        compiler_params=pltpu.CompilerParams(dimension_semantics=("parallel", "arbitrary"),
                                             vmem_limit_bytes=VMEM_LIMIT),
        name=name,
    )(*prefix_args, shape3(q), cache_k, cache_v, shape3(nk), shape3(nv), shape3(g))
    return out.reshape(batch * t, WIDTH)


def _rope_tables(pos):
    half = HEAD_DIM // 2
    inv = ROPE_THETA ** (-jnp.arange(half, dtype=F32) / half)
    ang = pos.astype(F32)[:, None] * inv[None, :]
    cos = jnp.cos(ang)
    sin = jnp.sin(ang)
    reps = PAIR // HEAD_DIM
    return (jnp.tile(jnp.concatenate([cos, cos], axis=1), (1, reps)),
            jnp.tile(jnp.concatenate([-sin, sin], axis=1), (1, reps)))


def kernel(x_prompt, x_sample, cache_da_k, cache_da_v, cache_sb_k, cache_sb_v, norm_w, w_in, lambda_qk,
           subln_w, w_proj_da, w_proj_sb, w_out, final_norm_w):
    batch, seq, d = x_prompt.shape
    dec_batch, dec_seq, _ = x_sample.shape
    depth, _, past_len, da_heads, _ = cache_da_k.shape
    sb_heads = cache_sb_k.shape[3]
    assert seq % ATTN_TILE == 0 and ATTN_TILE % CHUNK == 0 and seq % TOKEN_TILE == 0
    assert dec_batch * dec_seq <= TOKEN_TILE and past_len % CACHE_TILE == 0 and N_PAIRS * dec_seq <= PAIR
    assert da_heads == N_PAIRS and cache_da_k.shape[4] == PAIR and sb_heads * HEAD_DIM == WIDTH

    cos_p, sin_p = _rope_tables(jnp.arange(seq, dtype=jnp.int32))
    cos_s, sin_s = _rope_tables(jnp.tile(past_len + jnp.arange(dec_seq, dtype=jnp.int32), dec_batch))

    w_in_bf = w_in.astype(BF16)
    wkv_t_bf = jnp.swapaxes(w_in[:, :, G_KS * WIDTH:(G_VS + 1) * WIDTH], 1, 2).astype(BF16)
    wda_bf = w_proj_da.astype(BF16)
    wsb_bf = w_proj_sb.astype(BF16)
    wout_bf = w_out.astype(BF16)
    norm_w3 = norm_w.reshape(depth, 1, d)
    subln_w3 = subln_w.reshape(depth, 1, PAIR)
    fnw = final_norm_w.reshape(1, d)
    feature_major = lambda c: jnp.transpose(c, (0, 1, 3, 4, 2)).reshape(depth, dec_batch, WIDTH, past_len)
    cskt, csvt = feature_major(cache_sb_k), feature_major(cache_sb_v)
    head_rows = lambda c: c.reshape(depth, dec_batch, past_len * da_heads, PAIR)
    cdk, cdv = head_rows(cache_da_k), head_rows(cache_da_v)

    xp = x_prompt.reshape(batch * seq, d)
    xs = x_sample.reshape(dec_batch * dec_seq, d)
    kv_stack, rows_s = None, []
    for layer in range(depth):
        lam_init = 0.8 - 0.6 * math.exp(-0.3 * layer)
        lam_args = (lambda_qk, subln_w3)
        lam_specs = (pl.BlockSpec((None,) + lambda_qk.shape[1:], lambda b, i, layer=layer: (layer, 0, 0)),
                     pl.BlockSpec((None, 1, PAIR), lambda b, i, layer=layer: (layer, 0, 0)))
        last = layer == depth - 1

        (qd, kd, vd, gd, qs, kst, vst, gs, kdb, vdb, kstb, vstb) = _inproj(
            xp, norm_w3, w_in_bf, layer, cos_p, sin_p, wkv_t_bf, kv_stack)
        kv_stack = (kd, vd, kst, vst)
        rows2 = (N_PAIRS, 2 * ATTN_TILE)
        yda = _prompt_attention(
            functools.partial(_da_prompt_kernel, lam_init=lam_init), "da_prompt",
            batch, seq, qd, kdb, vdb, gd, False,
            [pltpu.VMEM(rows2 + (PAIR,), BF16), pltpu.VMEM(rows2 + (seq,), F32),
             pltpu.VMEM(rows2 + (PAIR,), F32), pltpu.VMEM(rows2 + (2 * PAIR,), F32)],
            lam_args, lam_specs)
        ysb = _prompt_attention(
            _sb_prompt_kernel, "sb_prompt", batch, seq, qs, kstb, vstb, gs, True,
            [pltpu.VMEM(rows2 + (PAIR,), BF16), pltpu.VMEM(rows2 + (1,), F32),
             pltpu.VMEM(rows2 + (PAIR,), F32)])
        xp = _post(xp, norm_w3, w_in_bf, yda, ysb, wda_bf, wsb_bf, wout_bf, fnw, layer, last)

        (qd, kd, vd, gd, qs, ks, vs, gs, kdb, vdb, ksb, vsb) = _inproj(
            xs, norm_w3, w_in_bf, layer, cos_s, sin_s)
        all_rows = N_PAIRS * 2 * dec_seq
        yda = _decode_attention(
            functools.partial(_da_decode_kernel, lam_init=lam_init, past_len=past_len), "da_decode",
            dec_seq, qd, cdk, cdv,
            pl.BlockSpec((None, None, CACHE_TILE * da_heads, PAIR), lambda b, j, layer=layer: (layer, b, j, 0)),
            kdb, vdb, gd,
            [pltpu.VMEM((all_rows, PAIR), BF16), pltpu.VMEM((all_rows, 1), F32),
             pltpu.VMEM((all_rows, 1), F32), pltpu.VMEM((all_rows, PAIR), F32)],
            lam_args, lam_specs)
        ysb = _sb_decode(layer, dec_seq, qs, cskt, csvt, ksb, vsb, gs)
        xs = _post(xs, norm_w3, w_in_bf, yda, ysb, wda_bf, wsb_bf, wout_bf, fnw, layer, last)
        rows_s.append((kd, vd, ks, vs))

    def stack(rows, idx, shape):
        return jnp.stack([r[idx] for r in rows], axis=0).reshape((depth,) + shape)

    def token_major(a):
        return jnp.transpose(a.reshape(depth, batch, sb_heads, HEAD_DIM, seq), (0, 1, 4, 2, 3))

    sb_s = (dec_batch, dec_seq, sb_heads, HEAD_DIM)
    da_s = (dec_batch, dec_seq, da_heads, PAIR)
    kd, vd, kst, vst = kv_stack
    return (xp.reshape(batch, seq, d), xs.reshape(dec_batch, dec_seq, d),
            kd.reshape(depth, batch, seq, da_heads, PAIR), vd.reshape(depth, batch, seq, da_heads, PAIR),
            token_major(kst), token_major(vst),
            stack(rows_s, 0, da_s), stack(rows_s, 1, da_s), stack(rows_s, 2, sb_s), stack(rows_s, 3, sb_s))
```

```python
import functools
import math

import jax
import jax.numpy as jnp
import numpy as np
from jax import lax
from jax.experimental import pallas as pl
from jax.experimental.pallas import tpu as pltpu

F32 = jnp.float32
BF16 = jnp.bfloat16

CHUNK = 64
HEAD_DIM = 64
PAIR = 2 * HEAD_DIM
N_PAIRS = 4
WIDTH = N_PAIRS * PAIR
ROPE_THETA = 10000.0
EPS = 1e-6
NEG_INF = -1e30
QK_SCALE = HEAD_DIM ** -0.5
LOG2E = math.log2(math.e)
SIGN_BIT = np.uint32(0x80000000)
DEAD_LOG = -150.0 * math.log(2.0)

TOKEN_TILE = 512
ATTN_TILE = 256
CACHE_TILE = 1024
VMEM_LIMIT = 56 * 1024 * 1024

G_QD, G_KD, G_VD, G_GD, G_QS, G_KS, G_VS, G_GS, G_GA = 0, 1, 2, 3, 4, 5, 6, 7, 8
KV_OUTPUTS = (1, 2, 5, 6)


def _sigmoid(x):
    return 1.0 / (1.0 + jnp.exp(-x))


def _nt_dot(a, b):
    return lax.dot_general(a, b, (((1,), (1,)), ((), ())), preferred_element_type=F32)


def _dot(a, b):
    return jnp.dot(a, b, preferred_element_type=F32)


def _inproj_kernel(*refs, feature_major_sb, n_aliased, zero_slots_except):
    if feature_major_sb:
        x_ref, nw_ref, w_ref, cos_ref, sin_ref, wkv_ref = refs[:6]
        outs = refs[6 + n_aliased:]
    else:
        x_ref, nw_ref, w_ref, cos_ref, sin_ref = refs[:5]
        outs = refs[5 + n_aliased:]
    qd_ref, kd_ref, vd_ref, gd_ref, qs_ref, ks_ref, vs_ref, gs_ref, kdb_ref, vdb_ref, ksb_ref, vsb_ref = outs
    if zero_slots_except is not None:
        stacked = (kd_ref, vd_ref, ks_ref, vs_ref)
        for ref in stacked:
            for slot in range(ref.shape[0]):
                if slot != zero_slots_except:
                    ref[slot] = jnp.zeros(ref.shape[1:], ref.dtype)
        kd_ref, vd_ref, ks_ref, vs_ref = (ref.at[zero_slots_except] for ref in stacked)

    x = x_ref[...]
    ms = jnp.mean(x * x, axis=-1, keepdims=True)
    h = (x * lax.rsqrt(ms + EPS) * nw_ref[...]).astype(BF16)

    def proj(group):
        return _dot(h, w_ref[:, group * WIDTH:(group + 1) * WIDTH])

    lane = lax.broadcasted_iota(jnp.int32, (1, PAIR), 1)
    upper_half = (lane & (HEAD_DIM // 2)) != 0
    cos = cos_ref[...]
    sin = sin_ref[...]

    def rope(s):
        partner = jnp.where(upper_half, pltpu.roll(s, HEAD_DIM // 2, 1),
                            pltpu.roll(s, PAIR - HEAD_DIM // 2, 1))
        return s * cos + partner * sin

    def slab(u, c):
        return u[:, c * PAIR:(c + 1) * PAIR]

    u = proj(G_QD)
    for c in range(N_PAIRS):
        qd_ref[:, c * PAIR:(c + 1) * PAIR] = (rope(slab(u, c)) * QK_SCALE).astype(BF16)
    tokens = x.shape[0]
    u = proj(G_KD)
    for c in range(N_PAIRS):
        kd = rope(slab(u, c))
        kd_ref[pl.ds(c, tokens, stride=N_PAIRS), :] = kd
        kdb_ref[:, c * PAIR:(c + 1) * PAIR] = kd.astype(BF16)
    u = proj(G_VD)
    for c in range(N_PAIRS):
        vd_ref[pl.ds(c, tokens, stride=N_PAIRS), :] = slab(u, c)
    vdb_ref[...] = u.astype(BF16)
    g = proj(G_GD)
    gd_ref[...] = (g * _sigmoid(g)).astype(BF16)
    qs_ref[...] = (proj(G_QS) * QK_SCALE).astype(BF16)
    if feature_major_sb:
        for half, (o_ref, ob_ref) in enumerate(((ks_ref, ksb_ref), (vs_ref, vsb_ref))):
            ut = _nt_dot(wkv_ref[half * WIDTH:(half + 1) * WIDTH, :], h)
            o_ref[...] = ut
            ob_ref[...] = ut.astype(BF16)
    else:
        for group, o_ref, ob_ref in ((G_KS, ks_ref, ksb_ref), (G_VS, vs_ref, vsb_ref)):
            u = proj(group)
            o_ref[...] = u
            ob_ref[...] = u.astype(BF16)
    g = proj(G_GS)
    gs_ref[...] = (g * _sigmoid(g)).astype(BF16)


def _inproj(x, norm_w, w_in_bf, layer, cos_tab, sin_tab, wkv_t_bf=None, kv_stack=None):
    n, d = x.shape
    creates_stack = wkv_t_bf is not None and kv_stack is None
    tm = min(TOKEN_TILE // 2 if creates_stack else TOKEN_TILE, n)
    seq = cos_tab.shape[0]
    n_tab = seq // tm
    depth, _, cols = w_in_bf.shape
    row = lambda i: (i, 0)
    wide = pl.BlockSpec((tm, WIDTH), row)
    full = pl.BlockSpec((tm, d), row)
    tab = pl.BlockSpec((tm, PAIR), lambda i: (i % n_tab, 0))
    weight = lambda r, c: pl.BlockSpec((None, r, c), lambda i: (layer, 0, 0), pipeline_mode=pl.Buffered(1))
    bf_out = jax.ShapeDtypeStruct((n, WIDTH), BF16)
    in_specs = [full, pl.BlockSpec((None, 1, d), lambda i: (layer, 0, 0)), weight(d, G_GA * WIDTH), tab, tab]
    args = [x, norm_w, w_in_bf, cos_tab, sin_tab]
    aliases = {}
    if wkv_t_bf is not None:
        in_specs.append(weight(2 * WIDTH, d))
        args.append(wkv_t_bf)
        slots, slot = (depth, 0) if creates_stack else (None, layer)
        heads = pl.BlockSpec((slots, tm * N_PAIRS, PAIR), lambda i: (slot, i, 0))
        f32_heads = jax.ShapeDtypeStruct((depth, n * N_PAIRS, PAIR), F32)
        sb_spec = pl.BlockSpec((slots, None, WIDTH, tm), lambda i: (slot, i // n_tab, 0, i % n_tab))
        sb_f32 = jax.ShapeDtypeStruct((depth, n // seq, WIDTH, seq), F32)
        sb_bf_spec = pl.BlockSpec((None, WIDTH, tm), lambda i: (i // n_tab, 0, i % n_tab))
        sb_bf = jax.ShapeDtypeStruct((n // seq, WIDTH, seq), BF16)
        if kv_stack is not None:
            for out_index, a in zip(KV_OUTPUTS, kv_stack):
                aliases[len(args)] = out_index
                in_specs.append(pl.BlockSpec(memory_space=pl.ANY))
                args.append(a)
    else:
        heads = pl.BlockSpec((tm * N_PAIRS, PAIR), lambda i: (i, 0))
        f32_heads = jax.ShapeDtypeStruct((n * N_PAIRS, PAIR), F32)
        sb_spec = sb_bf_spec = wide
        sb_f32 = jax.ShapeDtypeStruct((n, WIDTH), F32)
        sb_bf = bf_out
    return pl.pallas_call(
        functools.partial(_inproj_kernel, feature_major_sb=wkv_t_bf is not None, n_aliased=len(aliases),
                          zero_slots_except=layer if creates_stack else None),
        grid=(n // tm,),
        in_specs=in_specs,
        out_specs=[wide, heads, heads, wide, wide, sb_spec, sb_spec, wide, wide, wide, sb_bf_spec, sb_bf_spec],
        out_shape=[bf_out, f32_heads, f32_heads, bf_out, bf_out, sb_f32, sb_f32, bf_out,
                   bf_out, bf_out, sb_bf, sb_bf],
        input_output_aliases=aliases,
        compiler_params=pltpu.CompilerParams(dimension_semantics=("parallel",),
                                             vmem_limit_bytes=VMEM_LIMIT),
        name="inproj",
    )(*args)


def _post_kernel(x_ref, nw_ref, wg_ref, yda_ref, ysb_ref, wda_ref, wsb_ref, wout_ref, fnw_ref, o_ref,
                 *, final_norm):
    x = x_ref[...]
    d = x.shape[1]
    ms = jnp.mean(x * x, axis=-1, keepdims=True)
    h = (x * lax.rsqrt(ms + EPS) * nw_ref[...]).astype(BF16)
    merged = (_sigmoid(_dot(h, wg_ref[:, :d])) * _dot(yda_ref[...], wda_ref[...])
              + _sigmoid(_dot(h, wg_ref[:, d:])) * _dot(ysb_ref[...], wsb_ref[...])).astype(BF16)
    out = x + _dot(merged, wout_ref[...])
    if final_norm:
        ms = jnp.mean(out * out, axis=-1, keepdims=True)
        out = out * lax.rsqrt(ms + EPS) * fnw_ref[...]
    o_ref[...] = out


def _post(x, norm_w, w_in_bf, yda, ysb, wda_bf, wsb_bf, wout_bf, final_norm_w, layer, final_norm):
    n, d = x.shape
    tm = min(TOKEN_TILE, n)
    assert G_GA * WIDTH % (2 * d) == 0 and (G_GA * WIDTH + 2 * d) == w_in_bf.shape[2]
    row = lambda i: (i, 0)
    wide = pl.BlockSpec((tm, WIDTH), row)
    full = pl.BlockSpec((tm, d), row)
    weight = lambda r: pl.BlockSpec((None, r, d), lambda i: (layer, 0, 0), pipeline_mode=pl.Buffered(1))
    gate_weight = pl.BlockSpec((None, d, 2 * d), lambda i: (layer, 0, G_GA * WIDTH // (2 * d)),
                               pipeline_mode=pl.Buffered(1))
    return pl.pallas_call(
        functools.partial(_post_kernel, final_norm=final_norm),
        grid=(n // tm,),
        in_specs=[full, pl.BlockSpec((None, 1, d), lambda i: (layer, 0, 0)), gate_weight, wide, wide,
                  weight(WIDTH), weight(WIDTH), weight(d), pl.BlockSpec((1, d), lambda i: (0, 0))],
        out_specs=full,
        out_shape=jax.ShapeDtypeStruct((n, d), F32),
        compiler_params=pltpu.CompilerParams(dimension_semantics=("parallel",),
                                             vmem_limit_bytes=VMEM_LIMIT),
        name="post",
    )(x, norm_w, w_in_bf, yda, ysb, wda_bf, wsb_bf, wout_bf, final_norm_w)


def _split_pair(q):
    lane = lax.broadcasted_iota(jnp.int32, q.shape, 1)
    first = lane < HEAD_DIM
    zero = jnp.zeros_like(q)
    return jnp.concatenate([jnp.where(first, q, zero), jnp.where(first, zero, q)], axis=0)


def _lambda(lam_ref, lam_init):
    lf = lam_ref[...]
    a = jnp.sum(lf[0:1] * lf[1:2], axis=-1, keepdims=True)
    b = jnp.sum(lf[2:3] * lf[3:4], axis=-1, keepdims=True)
    return jnp.exp(a) - jnp.exp(b) + lam_init


def _softmax_step(s, v, carry):
    m, l, acc = carry
    m_new = jnp.maximum(m, jnp.max(s, axis=-1, keepdims=True))
    alpha = jnp.exp(m - m_new)
    p = jnp.exp(s - m_new)
    l = alpha * l + jnp.sum(p, axis=-1, keepdims=True)
    acc = alpha * acc + _dot(p.astype(BF16), v)
    return m_new, l, acc


def _da_finish(carry, lam, sw, gate, lam_init):
    _, l, acc = carry
    t = acc.shape[0] // 2
    o = acc[:t] * (1.0 / l[:t]) - lam * (acc[t:] * (1.0 / l[t:]))
    return _da_gate(o, sw, gate, lam_init)


def _da_gate(o, sw, gate, lam_init):
    ms = jnp.mean(o * o, axis=-1, keepdims=True)
    y = (o * lax.rsqrt(ms + EPS) * sw) * (1.0 - lam_init)
    return (y * gate.astype(F32)).astype(BF16)


def _neg_tri(n):
    j = lax.broadcasted_iota(jnp.int32, (n, n), 0)
    s = lax.broadcasted_iota(jnp.int32, (n, n), 1)
    return jnp.where(j >= s, -1.0, 0.0).astype(BF16)


def _sb_step(z, pv, neg_tri, carry, mask=None):
    run, acc = carry
    hi, lo, total = _sb_front(z, mask)
    c = _dot(hi, neg_tri) + _dot(lo, neg_tri)
    return run - total, acc + pv(_sb_back(z, c, run, mask))


def _sb_front(z, mask=None):
    neg_abs = lax.bitcast_convert_type(lax.bitcast_convert_type(z * LOG2E, jnp.uint32) | SIGN_BIT, F32)
    sp = jnp.maximum(z, 0.0) + jnp.log(1.0 + jnp.exp2(neg_abs))
    if mask is not None:
        sp = jnp.where(mask, sp, 0.0)
    hi = sp.astype(BF16)
    lo = (sp - hi.astype(F32)).astype(BF16)
    return hi, lo, jnp.sum(sp, axis=-1, keepdims=True)


def _sb_back(z, c, run, mask=None):
    if run.shape[1] != 1:
        run = jnp.concatenate([run] * (z.shape[1] // run.shape[1]), axis=1)
    a = jnp.exp2((z + c + run) * LOG2E)
    if mask is not None:
        a = jnp.where(mask, a, 0.0)
    return a.astype(BF16)


def _sb_alive(run):
    return (jnp.max(run) > DEAD_LOG).astype(jnp.int32)


def _sb_finish(carry, gate):
    _, acc = carry
    t = acc.shape[0] // 2
    lane = lax.broadcasted_iota(jnp.int32, (t, PAIR), 1)
    o = jnp.where(lane < HEAD_DIM, acc[:t], acc[t:])
    return (o * gate.astype(F32)).astype(BF16)


def _pair(h):
    return slice(h * PAIR, (h + 1) * PAIR)


def _da_prompt_kernel(lam_ref, sw_ref, q_ref, k_ref, v_ref, g_ref, o_ref, q2_ref, s_ref, mx_ref, acc_ref,
                      *, lam_init):
    i = pl.program_id(1)
    tq = q_ref.shape[0]
    row = lax.broadcasted_iota(jnp.int32, (2 * tq, tq), 0) % tq
    col = lax.broadcasted_iota(jnp.int32, (2 * tq, tq), 1)
    diag_mask = (col // CHUNK) <= (row // CHUNK)
    ones = jnp.ones((tq, PAIR), BF16)

    def scores(h, start, mask=None):
        s = _nt_dot(q2_ref[h], k_ref[pl.ds(start, tq), _pair(h)]) * LOG2E
        if mask is not None:
            s = jnp.where(mask, s, NEG_INF)
        s_ref[h, :, pl.ds(start, tq)] = s
        return jnp.maximum(s[:, :PAIR], s[:, PAIR:])

    for h in range(N_PAIRS):
        q2_ref[h] = _split_pair(q_ref[:, _pair(h)])
    for h in range(N_PAIRS):
        mx_ref[h] = scores(h, pl.multiple_of(i * tq, tq), diag_mask)

    @pl.loop(0, (i + 1) // 2)
    def _(kb2):
        for kb in (2 * kb2, jnp.minimum(2 * kb2 + 1, i - 1)):
            for h in range(N_PAIRS):
                mx_ref[h] = jnp.maximum(mx_ref[h], scores(h, pl.multiple_of(kb * tq, tq)))

    for h in range(N_PAIRS):
        m2 = jnp.max(mx_ref[h], axis=-1, keepdims=True)
        mx_ref[h] = jnp.broadcast_to(m2, mx_ref.shape[1:])

    def weigh(h, kb):
        start = pl.multiple_of(kb * tq, tq)
        m2 = mx_ref[h]
        p = jnp.exp2(s_ref[h, :, pl.ds(start, tq)] - jnp.concatenate([m2, m2], axis=1))
        v_ones = jnp.concatenate([v_ref[pl.ds(start, tq), _pair(h)], ones], axis=1)
        return _dot(p.astype(BF16), v_ones)

    for h in range(N_PAIRS):
        acc_ref[h] = weigh(h, i)

    @pl.loop(0, i // 2)
    def _(kb2):
        for h in range(N_PAIRS):
            acc_ref[h] += weigh(h, 2 * kb2) + weigh(h, 2 * kb2 + 1)

    @pl.when(i % 2 == 1)
    def _():
        for h in range(N_PAIRS):
            acc_ref[h] += weigh(h, i - 1)

    lam = _lambda(lam_ref, lam_init)
    sw = sw_ref[...]
    for h in range(N_PAIRS):
        acc = acc_ref[h]
        o = acc[:tq, :PAIR] / acc[:tq, PAIR:] - lam * (acc[tq:, :PAIR] / acc[tq:, PAIR:])
        o_ref[:, _pair(h)] = _da_gate(o, sw, g_ref[:, _pair(h)], lam_init)


def _sb_prompt_kernel(q_ref, kt_ref, vt_ref, g_ref, o_ref, q2_ref, run_ref, acc_ref):
    i = pl.program_id(1)
    tq = q_ref.shape[0]
    row = lax.broadcasted_iota(jnp.int32, (2 * tq, tq), 0) % tq
    col = lax.broadcasted_iota(jnp.int32, (2 * tq, tq), 1)
    diag_mask = col < row
    neg_tri = _neg_tri(tq)

    def block(start, mask=None):
        for h in range(N_PAIRS):
            z = _dot(q2_ref[h], kt_ref[_pair(h), pl.ds(start, tq)])
            run, acc = _sb_step(z, lambda a: _nt_dot(a, vt_ref[_pair(h), pl.ds(start, tq)]), neg_tri,
                                (run_ref[h], acc_ref[h]), mask)
            run_ref[h] = run
            acc_ref[h] = acc

    run_ref[...] = jnp.zeros(run_ref.shape, F32)
    acc_ref[...] = jnp.zeros(acc_ref.shape, F32)
    for h in range(N_PAIRS):
        q2_ref[h] = _split_pair(q_ref[:, _pair(h)])
    block(pl.multiple_of(i * tq, tq), diag_mask)

    def earlier_block(state):
        kb, _ = state
        block(pl.multiple_of(kb * tq, tq))
        return kb - 1, _sb_alive(run_ref[...])

    lax.while_loop(lambda state: (state[0] >= 0) & (state[1] > 0), earlier_block,
                   (i - 1, _sb_alive(run_ref[...])))
    for h in range(N_PAIRS):
        o_ref[:, _pair(h)] = _sb_finish((run_ref[h], acc_ref[h]), g_ref[:, _pair(h)])


def _prompt_attention(kernel_fn, name, batch, seq, q, k, v, g, feature_major_kv, scratch,
                      prefix_args=(), prefix_specs=()):
    tq = ATTN_TILE
    qspec = pl.BlockSpec((None, tq, WIDTH), lambda b, i: (b, i, 0))
    shape3 = lambda a: a.reshape(batch, seq, WIDTH)
    if feature_major_kv:
        kvspec = pl.BlockSpec((None, WIDTH, seq), lambda b, i: (b, 0, 0))
    else:
        kvspec = pl.BlockSpec((None, seq, WIDTH), lambda b, i: (b, 0, 0))
        k, v = shape3(k), shape3(v)
    out = pl.pallas_call(
        kernel_fn,
        grid=(batch, seq // tq),
        in_specs=list(prefix_specs) + [qspec, kvspec, kvspec, qspec],
        out_specs=qspec,
        out_shape=jax.ShapeDtypeStruct((batch, seq, WIDTH), BF16),
        scratch_shapes=scratch,
        compiler_params=pltpu.CompilerParams(dimension_semantics=("parallel", "arbitrary"),
                                             vmem_limit_bytes=VMEM_LIMIT),
        name=name,
    )(*prefix_args, shape3(q), k, v, shape3(g))
    return out.reshape(batch * seq, WIDTH)


def _pad_rows(a, rows):
    return jnp.concatenate([a, jnp.zeros((rows - a.shape[0], a.shape[1]), a.dtype)], axis=0)


def _da_decode_kernel(lam_ref, sw_ref, q_ref, ck_ref, cv_ref, nk_ref, nv_ref, g_ref, o_ref,
                      q_all_ref, m_ref, l_ref, acc_ref, *, lam_init, past_len):
    j = pl.program_id(1)
    t = q_ref.shape[0]
    n_rows = N_PAIRS * 2 * t
    q_head = lax.broadcasted_iota(jnp.int32, (n_rows, 1), 0) // (2 * t)

    @pl.when(j == 0)
    def _():
        m_ref[...] = jnp.full(m_ref.shape, NEG_INF, F32)
        l_ref[...] = jnp.zeros(l_ref.shape, F32)
        acc_ref[...] = jnp.zeros(acc_ref.shape, F32)
        q_all_ref[...] = jnp.concatenate([_split_pair(q_ref[:, _pair(h)]) for h in range(N_PAIRS)], axis=0)

    key_head = lax.broadcasted_iota(jnp.int32, (1, ck_ref.shape[0]), 1) % N_PAIRS
    s = jnp.where(key_head == q_head, _nt_dot(q_all_ref[...], ck_ref[...].astype(BF16)), NEG_INF)
    m, l, acc = _softmax_step(s, cv_ref[...].astype(BF16), (m_ref[...], l_ref[...], acc_ref[...]))
    m_ref[...] = m
    l_ref[...] = l
    acc_ref[...] = acc

    @pl.when(j == pl.num_programs(1) - 1)
    def _():
        lam = _lambda(lam_ref, lam_init)
        sw = sw_ref[...]
        stack = lambda ref: _pad_rows(jnp.concatenate([ref[:, _pair(h)] for h in range(N_PAIRS)], axis=0), PAIR)
        k_idx = lax.broadcasted_iota(jnp.int32, (1, PAIR), 1)
        q_pos = past_len + lax.broadcasted_iota(jnp.int32, (n_rows, 1), 0) % t
        mask = ((k_idx // t == q_head) & (((past_len + k_idx % t) // CHUNK) <= (q_pos // CHUNK)))
        s = jnp.where(mask, _nt_dot(q_all_ref[...], stack(nk_ref)), NEG_INF)
        m, l, acc = _softmax_step(s, stack(nv_ref), (m_ref[...], l_ref[...], acc_ref[...]))
        for h in range(N_PAIRS):
            rows = slice(h * 2 * t, (h + 1) * 2 * t)
            o_ref[:, _pair(h)] = _da_finish((m[rows], l[rows], acc[rows]), lam, sw, g_ref[:, _pair(h)], lam_init)


def _sb_decode_kernel(q_ref, ckt_ref, cvt_ref, g_ref, in_a_ref, in_b_ref, o_ref, run_out_ref, acc_out_ref,
                      run_ref, acc_ref, *, first):
    j = pl.program_id(1)
    t = q_ref.shape[0]
    sub = ATTN_TILE
    n_sub = ckt_ref.shape[1] // sub
    neg_tri = _neg_tri(sub)

    @pl.when(j == 0)
    def _():
        if first:
            q_idx = lax.broadcasted_iota(jnp.int32, (2 * t, PAIR), 0) % t
            k_idx = lax.broadcasted_iota(jnp.int32, (2 * t, PAIR), 1)
            mask = k_idx < q_idx
            for h in range(N_PAIRS):
                q2 = _split_pair(q_ref[:, _pair(h)])
                nv = _pad_rows(in_b_ref[:, _pair(h)], PAIR)
                init = (jnp.zeros((2 * t, 1), F32), jnp.zeros((2 * t, PAIR), F32))
                run, acc = _sb_step(_nt_dot(q2, _pad_rows(in_a_ref[:, _pair(h)], PAIR)), lambda a: _dot(a, nv),
                                    neg_tri[:PAIR, :PAIR], init, mask)
                run_ref[h] = run
                acc_ref[h] = acc
        else:
            run_ref[...] = in_a_ref[...]
            acc_ref[...] = in_b_ref[...]

    for c in reversed(range(n_sub)):
        cols = slice(c * sub, (c + 1) * sub)

        @pl.when(_sb_alive(run_ref[...]) > 0)
        def _():
            for h in range(N_PAIRS):
                q2 = _split_pair(q_ref[:, _pair(h)])
                z = _dot(q2, ckt_ref[_pair(h), cols].astype(BF16))
                run, acc = _sb_step(z, lambda a: _nt_dot(a, cvt_ref[_pair(h), cols].astype(BF16)), neg_tri,
                                    (run_ref[h], acc_ref[h]))
                run_ref[h] = run
                acc_ref[h] = acc

    @pl.when(j == pl.num_programs(1) - 1)
    def _():
        run_out_ref[...] = run_ref[...]
        acc_out_ref[...] = acc_ref[...]
        for h in range(N_PAIRS):
            o_ref[:, _pair(h)] = _sb_finish((run_ref[h], acc_ref[h]), g_ref[:, _pair(h)])


def _sb_decode(layer, t, q, cache_kt, cache_vt, nk, nv, g):
    batch, past = cache_kt.shape[1], cache_kt.shape[3]
    n_blocks = past // CACHE_TILE
    tok = pl.BlockSpec((None, t, WIDTH), lambda b, j: (b, 0, 0))
    run_spec = pl.BlockSpec((None, N_PAIRS, 2 * t, 1), lambda b, j: (b, 0, 0, 0))
    acc_spec = pl.BlockSpec((None, N_PAIRS, 2 * t, PAIR), lambda b, j: (b, 0, 0, 0))
    run_shape = jax.ShapeDtypeStruct((batch, N_PAIRS, 2 * t, 1), F32)
    acc_shape = jax.ShapeDtypeStruct((batch, N_PAIRS, 2 * t, PAIR), F32)
    shape3 = lambda a: a.reshape(batch, t, WIDTH)

    def call(first, steps, newest_block, in_a, in_b, in_spec_a, in_spec_b):
        cache = pl.BlockSpec((None, None, WIDTH, CACHE_TILE), lambda b, j: (layer, b, 0, newest_block - j))
        return pl.pallas_call(
            functools.partial(_sb_decode_kernel, first=first),
            grid=(batch, steps),
            in_specs=[tok, cache, cache, tok, in_spec_a, in_spec_b],
            out_specs=[tok, run_spec, acc_spec],
            out_shape=[jax.ShapeDtypeStruct((batch, t, WIDTH), BF16), run_shape, acc_shape],
            scratch_shapes=[pltpu.VMEM((N_PAIRS, 2 * t, 1), F32), pltpu.VMEM((N_PAIRS, 2 * t, PAIR), F32)],
            compiler_params=pltpu.CompilerParams(dimension_semantics=("parallel", "arbitrary"),
                                                 vmem_limit_bytes=VMEM_LIMIT),
            name="sb_decode",
        )(shape3(q), cache_kt, cache_vt, shape3(g), in_a, in_b)

    y, run, acc = call(True, 1, n_blocks - 1, shape3(nk), shape3(nv), tok, tok)
    if n_blocks > 1:
        y = lax.cond(jnp.max(run) > DEAD_LOG,
                     lambda: call(False, n_blocks - 1, n_blocks - 2, run, acc, run_spec, acc_spec)[0],
                     lambda: y)
    return y.reshape(batch * t, WIDTH)


def _decode_attention(kernel_fn, name, t, q, cache_k, cache_v, cache_spec, nk, nv, g, scratch,
                      prefix_args=(), prefix_specs=()):
    batch = cache_k.shape[1]
    tok = pl.BlockSpec((None, t, WIDTH), lambda b, j: (b, 0, 0))
    shape3 = lambda a: a.reshape(batch, t, WIDTH)
    out = pl.pallas_call(
        kernel_fn,
        grid=(batch, cache_k.size // (cache_k.shape[0] * batch * WIDTH * CACHE_TILE)),
        in_specs=list(prefix_specs) + [tok, cache_spec, cache_spec, tok, tok, tok],
        out_specs=tok,
        out_shape=jax.ShapeDtypeStruct((batch, t, WIDTH), BF16),
        scratch_shapes=scratch,
        compiler_params=pltpu.CompilerParams(dimension_semantics=("parallel", "arbitrary"),
                                             vmem_limit_bytes=VMEM_LIMIT),
        name=name,
    )(*prefix_args, shape3(q), cache_k, cache_v, shape3(nk), shape3(nv), shape3(g))
    return out.reshape(batch * t, WIDTH)


def _rope_tables(pos):
    half = HEAD_DIM // 2
    inv = ROPE_THETA ** (-jnp.arange(half, dtype=F32) / half)
    ang = pos.astype(F32)[:, None] * inv[None, :]
    cos = jnp.cos(ang)
    sin = jnp.sin(ang)
    reps = PAIR // HEAD_DIM
    return (jnp.tile(jnp.concatenate([cos, cos], axis=1), (1, reps)),
            jnp.tile(jnp.concatenate([-sin, sin], axis=1), (1, reps)))


def kernel(x_prompt, x_sample, cache_da_k, cache_da_v, cache_sb_k, cache_sb_v, norm_w, w_in, lambda_qk,
           subln_w, w_proj_da, w_proj_sb, w_out, final_norm_w):
    batch, seq, d = x_prompt.shape
    dec_batch, dec_seq, _ = x_sample.shape
    depth, _, past_len, da_heads, _ = cache_da_k.shape
    sb_heads = cache_sb_k.shape[3]
    assert seq % ATTN_TILE == 0 and ATTN_TILE % CHUNK == 0 and seq % TOKEN_TILE == 0
    assert dec_batch * dec_seq <= TOKEN_TILE and past_len % CACHE_TILE == 0 and N_PAIRS * dec_seq <= PAIR
    assert da_heads == N_PAIRS and cache_da_k.shape[4] == PAIR and sb_heads * HEAD_DIM == WIDTH

    cos_p, sin_p = _rope_tables(jnp.arange(seq, dtype=jnp.int32))
    cos_s, sin_s = _rope_tables(jnp.tile(past_len + jnp.arange(dec_seq, dtype=jnp.int32), dec_batch))

    w_in_bf = w_in.astype(BF16)
    wkv_t_bf = jnp.swapaxes(w_in[:, :, G_KS * WIDTH:(G_VS + 1) * WIDTH], 1, 2).astype(BF16)
    wda_bf = w_proj_da.astype(BF16)
    wsb_bf = w_proj_sb.astype(BF16)
    wout_bf = w_out.astype(BF16)
    norm_w3 = norm_w.reshape(depth, 1, d)
    subln_w3 = subln_w.reshape(depth, 1, PAIR)
    fnw = final_norm_w.reshape(1, d)
    feature_major = lambda c: jnp.transpose(c, (0, 1, 3, 4, 2)).reshape(depth, dec_batch, WIDTH, past_len)
    cskt, csvt = feature_major(cache_sb_k), feature_major(cache_sb_v)
    head_rows = lambda c: c.reshape(depth, dec_batch, past_len * da_heads, PAIR)
    cdk, cdv = head_rows(cache_da_k), head_rows(cache_da_v)

    xp = x_prompt.reshape(batch * seq, d)
    xs = x_sample.reshape(dec_batch * dec_seq, d)
    kv_stack, rows_s = None, []
    for layer in range(depth):
        lam_init = 0.8 - 0.6 * math.exp(-0.3 * layer)
        lam_args = (lambda_qk, subln_w3)
        lam_specs = (pl.BlockSpec((None,) + lambda_qk.shape[1:], lambda b, i, layer=layer: (layer, 0, 0)),
                     pl.BlockSpec((None, 1, PAIR), lambda b, i, layer=layer: (layer, 0, 0)))
        last = layer == depth - 1

        (qd, kd, vd, gd, qs, kst, vst, gs, kdb, vdb, kstb, vstb) = _inproj(
            xp, norm_w3, w_in_bf, layer, cos_p, sin_p, wkv_t_bf, kv_stack)
        kv_stack = (kd, vd, kst, vst)
        rows2 = (N_PAIRS, 2 * ATTN_TILE)
        yda = _prompt_attention(
            functools.partial(_da_prompt_kernel, lam_init=lam_init), "da_prompt",
            batch, seq, qd, kdb, vdb, gd, False,
            [pltpu.VMEM(rows2 + (PAIR,), BF16), pltpu.VMEM(rows2 + (seq,), F32),
             pltpu.VMEM(rows2 + (PAIR,), F32), pltpu.VMEM(rows2 + (2 * PAIR,), F32)],
            lam_args, lam_specs)
        ysb = _prompt_attention(
            _sb_prompt_kernel, "sb_prompt", batch, seq, qs, kstb, vstb, gs, True,
            [pltpu.VMEM(rows2 + (PAIR,), BF16), pltpu.VMEM(rows2 + (1,), F32),
             pltpu.VMEM(rows2 + (PAIR,), F32)])
        xp = _post(xp, norm_w3, w_in_bf, yda, ysb, wda_bf, wsb_bf, wout_bf, fnw, layer, last)

        (qd, kd, vd, gd, qs, ks, vs, gs, kdb, vdb, ksb, vsb) = _inproj(
            xs, norm_w3, w_in_bf, layer, cos_s, sin_s)
        all_rows = N_PAIRS * 2 * dec_seq
        yda = _decode_attention(
            functools.partial(_da_decode_kernel, lam_init=lam_init, past_len=past_len), "da_decode",
            dec_seq, qd, cdk, cdv,
            pl.BlockSpec((None, None, CACHE_TILE * da_heads, PAIR), lambda b, j, layer=layer: (layer, b, j, 0)),
            kdb, vdb, gd,
            [pltpu.VMEM((all_rows, PAIR), BF16), pltpu.VMEM((all_rows, 1), F32),
             pltpu.VMEM((all_rows, 1), F32), pltpu.VMEM((all_rows, PAIR), F32)],
            lam_args, lam_specs)
        ysb = _sb_decode(layer, dec_seq, qs, cskt, csvt, ksb, vsb, gs)
        xs = _post(xs, norm_w3, w_in_bf, yda, ysb, wda_bf, wsb_bf, wout_bf, fnw, layer, last)
        rows_s.append((kd, vd, ks, vs))

    def stack(rows, idx, shape):
        return jnp.stack([r[idx] for r in rows], axis=0).reshape((depth,) + shape)

    def token_major(a):
        return jnp.transpose(a.reshape(depth, batch, sb_heads, HEAD_DIM, seq), (0, 1, 4, 2, 3))

    sb_s = (dec_batch, dec_seq, sb_heads, HEAD_DIM)
    da_s = (dec_batch, dec_seq, da_heads, PAIR)
    kd, vd, kst, vst = kv_stack
    return (xp.reshape(batch, seq, d), xs.reshape(dec_batch, dec_seq, d),
            kd.reshape(depth, batch, seq, da_heads, PAIR), vd.reshape(depth, batch, seq, da_heads, PAIR),
            token_major(kst), token_major(vst),
            stack(rows_s, 0, da_s), stack(rows_s, 1, da_s), stack(rows_s, 2, sb_s), stack(rows_s, 3, sb_s))
```

```python
import functools
import math

import jax
import jax.numpy as jnp
import numpy as np
from jax import lax
from jax.experimental import pallas as pl
from jax.experimental.pallas import tpu as pltpu

F32 = jnp.float32
BF16 = jnp.bfloat16

CHUNK = 64
HEAD_DIM = 64
PAIR = 2 * HEAD_DIM
N_PAIRS = 4
WIDTH = N_PAIRS * PAIR
ROPE_THETA = 10000.0
EPS = 1e-6
NEG_INF = -1e30
QK_SCALE = HEAD_DIM ** -0.5
LOG2E = math.log2(math.e)
SIGN_BIT = np.uint32(0x80000000)
DEAD_LOG = -150.0 * math.log(2.0)

TOKEN_TILE = 512
ATTN_TILE = 256
SB_EARLY_ROWS = 192
CACHE_TILE = 1024
VMEM_LIMIT = 56 * 1024 * 1024

G_QD, G_KD, G_VD, G_GD, G_QS, G_KS, G_VS, G_GS, G_GA = 0, 1, 2, 3, 4, 5, 6, 7, 8
KV_OUTPUTS = (1, 2, 5, 6)


def _sigmoid(x):
    return 1.0 / (1.0 + jnp.exp(-x))


def _nt_dot(a, b):
    return lax.dot_general(a, b, (((1,), (1,)), ((), ())), preferred_element_type=F32)


def _dot(a, b):
    return jnp.dot(a, b, preferred_element_type=F32)


def _inproj_kernel(*refs, feature_major_sb, n_aliased, zero_slots_except):
    if feature_major_sb:
        x_ref, nw_ref, w_ref, cos_ref, sin_ref, wkv_ref = refs[:6]
        outs = refs[6 + n_aliased:]
    else:
        x_ref, nw_ref, w_ref, cos_ref, sin_ref = refs[:5]
        outs = refs[5 + n_aliased:]
    qd_ref, kd_ref, vd_ref, gd_ref, qs_ref, ks_ref, vs_ref, gs_ref, kdb_ref, vdb_ref, ksb_ref, vsb_ref = outs
    if zero_slots_except is not None:
        stacked = (kd_ref, vd_ref, ks_ref, vs_ref)
        for ref in stacked:
            for slot in range(ref.shape[0]):
                if slot != zero_slots_except:
                    ref[slot] = jnp.zeros(ref.shape[1:], ref.dtype)
        kd_ref, vd_ref, ks_ref, vs_ref = (ref.at[zero_slots_except] for ref in stacked)

    x = x_ref[...]
    ms = jnp.mean(x * x, axis=-1, keepdims=True)
    h = (x * lax.rsqrt(ms + EPS) * nw_ref[...]).astype(BF16)

    def proj(group):
        return _dot(h, w_ref[:, group * WIDTH:(group + 1) * WIDTH])

    lane = lax.broadcasted_iota(jnp.int32, (1, PAIR), 1)
    upper_half = (lane & (HEAD_DIM // 2)) != 0
    cos = cos_ref[...]
    sin = sin_ref[...]

    def rope(s):
        partner = jnp.where(upper_half, pltpu.roll(s, HEAD_DIM // 2, 1),
                            pltpu.roll(s, PAIR - HEAD_DIM // 2, 1))
        return s * cos + partner * sin

    def slab(u, c):
        return u[:, c * PAIR:(c + 1) * PAIR]

    u = proj(G_QD)
    for c in range(N_PAIRS):
        qd_ref[:, c * PAIR:(c + 1) * PAIR] = (rope(slab(u, c)) * QK_SCALE).astype(BF16)
    tokens = x.shape[0]
    u = proj(G_KD)
    for c in range(N_PAIRS):
        kd = rope(slab(u, c))
        kd_ref[pl.ds(c, tokens, stride=N_PAIRS), :] = kd
        kdb_ref[:, c * PAIR:(c + 1) * PAIR] = kd.astype(BF16)
    u = proj(G_VD)
    for c in range(N_PAIRS):
        vd_ref[pl.ds(c, tokens, stride=N_PAIRS), :] = slab(u, c)
    vdb_ref[...] = u.astype(BF16)
    g = proj(G_GD)
    gd_ref[...] = (g * _sigmoid(g)).astype(BF16)
    qs_ref[...] = (proj(G_QS) * QK_SCALE).astype(BF16)
    if feature_major_sb:
        for half, (o_ref, ob_ref) in enumerate(((ks_ref, ksb_ref), (vs_ref, vsb_ref))):
            ut = _nt_dot(wkv_ref[half * WIDTH:(half + 1) * WIDTH, :], h)
            o_ref[...] = ut
            ob_ref[...] = ut.astype(BF16)
    else:
        for group, o_ref, ob_ref in ((G_KS, ks_ref, ksb_ref), (G_VS, vs_ref, vsb_ref)):
            u = proj(group)
            o_ref[...] = u
            ob_ref[...] = u.astype(BF16)
    g = proj(G_GS)
    gs_ref[...] = (g * _sigmoid(g)).astype(BF16)


def _inproj(x, norm_w, w_in_bf, layer, cos_tab, sin_tab, wkv_t_bf=None, kv_stack=None):
    n, d = x.shape
    creates_stack = wkv_t_bf is not None and kv_stack is None
    tm = min(TOKEN_TILE // 2 if creates_stack else TOKEN_TILE, n)
    seq = cos_tab.shape[0]
    n_tab = seq // tm
    depth, _, cols = w_in_bf.shape
    row = lambda i: (i, 0)
    wide = pl.BlockSpec((tm, WIDTH), row)
    full = pl.BlockSpec((tm, d), row)
    tab = pl.BlockSpec((tm, PAIR), lambda i: (i % n_tab, 0))
    weight = lambda r, c: pl.BlockSpec((None, r, c), lambda i: (layer, 0, 0), pipeline_mode=pl.Buffered(1))
    bf_out = jax.ShapeDtypeStruct((n, WIDTH), BF16)
    in_specs = [full, pl.BlockSpec((None, 1, d), lambda i: (layer, 0, 0)), weight(d, G_GA * WIDTH), tab, tab]
    args = [x, norm_w, w_in_bf, cos_tab, sin_tab]
    aliases = {}
    if wkv_t_bf is not None:
        in_specs.append(weight(2 * WIDTH, d))
        args.append(wkv_t_bf)
        slots, slot = (depth, 0) if creates_stack else (None, layer)
        heads = pl.BlockSpec((slots, tm * N_PAIRS, PAIR), lambda i: (slot, i, 0))
        f32_heads = jax.ShapeDtypeStruct((depth, n * N_PAIRS, PAIR), F32)
        sb_spec = pl.BlockSpec((slots, None, WIDTH, tm), lambda i: (slot, i // n_tab, 0, i % n_tab))
        sb_f32 = jax.ShapeDtypeStruct((depth, n // seq, WIDTH, seq), F32)
        sb_bf_spec = pl.BlockSpec((None, WIDTH, tm), lambda i: (i // n_tab, 0, i % n_tab))
        sb_bf = jax.ShapeDtypeStruct((n // seq, WIDTH, seq), BF16)
        if kv_stack is not None:
            for out_index, a in zip(KV_OUTPUTS, kv_stack):
                aliases[len(args)] = out_index
                in_specs.append(pl.BlockSpec(memory_space=pl.ANY))
                args.append(a)
    else:
        heads = pl.BlockSpec((tm * N_PAIRS, PAIR), lambda i: (i, 0))
        f32_heads = jax.ShapeDtypeStruct((n * N_PAIRS, PAIR), F32)
        sb_spec = sb_bf_spec = wide
        sb_f32 = jax.ShapeDtypeStruct((n, WIDTH), F32)
        sb_bf = bf_out
    return pl.pallas_call(
        functools.partial(_inproj_kernel, feature_major_sb=wkv_t_bf is not None, n_aliased=len(aliases),
                          zero_slots_except=layer if creates_stack else None),
        grid=(n // tm,),
        in_specs=in_specs,
        out_specs=[wide, heads, heads, wide, wide, sb_spec, sb_spec, wide, wide, wide, sb_bf_spec, sb_bf_spec],
        out_shape=[bf_out, f32_heads, f32_heads, bf_out, bf_out, sb_f32, sb_f32, bf_out,
                   bf_out, bf_out, sb_bf, sb_bf],
        input_output_aliases=aliases,
        compiler_params=pltpu.CompilerParams(dimension_semantics=("parallel",),
                                             vmem_limit_bytes=VMEM_LIMIT),
        name="inproj",
    )(*args)


def _post_kernel(x_ref, nw_ref, wg_ref, yda_ref, ysb_ref, wda_ref, wsb_ref, wout_ref, fnw_ref, o_ref,
                 *, final_norm):
    x = x_ref[...]
    d = x.shape[1]
    ms = jnp.mean(x * x, axis=-1, keepdims=True)
    h = (x * lax.rsqrt(ms + EPS) * nw_ref[...]).astype(BF16)
    merged = (_sigmoid(_dot(h, wg_ref[:, :d])) * _dot(yda_ref[...], wda_ref[...])
              + _sigmoid(_dot(h, wg_ref[:, d:])) * _dot(ysb_ref[...], wsb_ref[...])).astype(BF16)
    out = x + _dot(merged, wout_ref[...])
    if final_norm:
        ms = jnp.mean(out * out, axis=-1, keepdims=True)
        out = out * lax.rsqrt(ms + EPS) * fnw_ref[...]
    o_ref[...] = out


def _post(x, norm_w, w_in_bf, yda, ysb, wda_bf, wsb_bf, wout_bf, final_norm_w, layer, final_norm):
    n, d = x.shape
    tm = min(TOKEN_TILE, n)
    assert G_GA * WIDTH % (2 * d) == 0 and (G_GA * WIDTH + 2 * d) == w_in_bf.shape[2]
    row = lambda i: (i, 0)
    wide = pl.BlockSpec((tm, WIDTH), row)
    full = pl.BlockSpec((tm, d), row)
    weight = lambda r: pl.BlockSpec((None, r, d), lambda i: (layer, 0, 0), pipeline_mode=pl.Buffered(1))
    gate_weight = pl.BlockSpec((None, d, 2 * d), lambda i: (layer, 0, G_GA * WIDTH // (2 * d)),
                               pipeline_mode=pl.Buffered(1))
    return pl.pallas_call(
        functools.partial(_post_kernel, final_norm=final_norm),
        grid=(n // tm,),
        in_specs=[full, pl.BlockSpec((None, 1, d), lambda i: (layer, 0, 0)), gate_weight, wide, wide,
                  weight(WIDTH), weight(WIDTH), weight(d), pl.BlockSpec((1, d), lambda i: (0, 0))],
        out_specs=full,
        out_shape=jax.ShapeDtypeStruct((n, d), F32),
        compiler_params=pltpu.CompilerParams(dimension_semantics=("parallel",),
                                             vmem_limit_bytes=VMEM_LIMIT),
        name="post",
    )(x, norm_w, w_in_bf, yda, ysb, wda_bf, wsb_bf, wout_bf, final_norm_w)


def _split_pair(q):
    lane = lax.broadcasted_iota(jnp.int32, q.shape, 1)
    first = lane < HEAD_DIM
    zero = jnp.zeros_like(q)
    return jnp.concatenate([jnp.where(first, q, zero), jnp.where(first, zero, q)], axis=0)


def _lambda(lam_ref, lam_init):
    lf = lam_ref[...]
    a = jnp.sum(lf[0:1] * lf[1:2], axis=-1, keepdims=True)
    b = jnp.sum(lf[2:3] * lf[3:4], axis=-1, keepdims=True)
    return jnp.exp(a) - jnp.exp(b) + lam_init


def _softmax_step(s, v, carry):
    m, l, acc = carry
    m_new = jnp.maximum(m, jnp.max(s, axis=-1, keepdims=True))
    alpha = jnp.exp(m - m_new)
    p = jnp.exp(s - m_new)
    l = alpha * l + jnp.sum(p, axis=-1, keepdims=True)
    acc = alpha * acc + _dot(p.astype(BF16), v)
    return m_new, l, acc


def _da_finish(carry, lam, sw, gate, lam_init):
    _, l, acc = carry
    t = acc.shape[0] // 2
    o = acc[:t] * (1.0 / l[:t]) - lam * (acc[t:] * (1.0 / l[t:]))
    return _da_gate(o, sw, gate, lam_init)


def _da_gate(o, sw, gate, lam_init):
    ms = jnp.mean(o * o, axis=-1, keepdims=True)
    y = (o * lax.rsqrt(ms + EPS) * sw) * (1.0 - lam_init)
    return (y * gate.astype(F32)).astype(BF16)


def _neg_tri(n):
    j = lax.broadcasted_iota(jnp.int32, (n, n), 0)
    s = lax.broadcasted_iota(jnp.int32, (n, n), 1)
    return jnp.where(j >= s, -1.0, 0.0).astype(BF16)


def _sb_step(z, pv, neg_tri, carry, mask=None):
    run, acc = carry
    hi, lo, total = _sb_front(z, mask)
    c = _dot(hi, neg_tri) + _dot(lo, neg_tri)
    return run - total, acc + pv(_sb_back(z, c, run, mask))


def _sb_front(z, mask=None):
    neg_abs = lax.bitcast_convert_type(lax.bitcast_convert_type(z * LOG2E, jnp.uint32) | SIGN_BIT, F32)
    sp = jnp.maximum(z, 0.0) + jnp.log(1.0 + jnp.exp2(neg_abs))
    if mask is not None:
        sp = jnp.where(mask, sp, 0.0)
    hi = sp.astype(BF16)
    lo = (sp - hi.astype(F32)).astype(BF16)
    return hi, lo, jnp.sum(sp, axis=-1, keepdims=True)


def _sb_back(z, c, run, mask=None):
    if run.shape[1] != 1:
        run = jnp.concatenate([run] * (z.shape[1] // run.shape[1]), axis=1)
    a = jnp.exp2((z + c + run) * LOG2E)
    if mask is not None:
        a = jnp.where(mask, a, 0.0)
    return a.astype(BF16)


def _sb_alive(run):
    return (jnp.max(run) > DEAD_LOG).astype(jnp.int32)


def _sb_finish(carry, gate):
    _, acc = carry
    t = acc.shape[0] // 2
    lane = lax.broadcasted_iota(jnp.int32, (t, PAIR), 1)
    o = jnp.where(lane < HEAD_DIM, acc[:t], acc[t:])
    return (o * gate.astype(F32)).astype(BF16)


def _pair(h):
    return slice(h * PAIR, (h + 1) * PAIR)


def _da_prompt_kernel(lam_ref, sw_ref, q_ref, k_ref, v_ref, g_ref, o_ref, q2_ref, s_ref, mx_ref, acc_ref,
                      *, lam_init):
    i = pl.program_id(1)
    tq = q_ref.shape[0]
    row = lax.broadcasted_iota(jnp.int32, (2 * tq, tq), 0) % tq
    col = lax.broadcasted_iota(jnp.int32, (2 * tq, tq), 1)
    diag_mask = (col // CHUNK) <= (row // CHUNK)
    ones = jnp.ones((tq, PAIR), BF16)

    def scores(h, start, mask=None):
        s = _nt_dot(q2_ref[h], k_ref[pl.ds(start, tq), _pair(h)]) * LOG2E
        if mask is not None:
            s = jnp.where(mask, s, NEG_INF)
        s_ref[h, :, pl.ds(start, tq)] = s
        return jnp.maximum(s[:, :PAIR], s[:, PAIR:])

    for h in range(N_PAIRS):
        q2_ref[h] = _split_pair(q_ref[:, _pair(h)])
    for h in range(N_PAIRS):
        mx_ref[h] = scores(h, pl.multiple_of(i * tq, tq), diag_mask)

    @pl.loop(0, (i + 1) // 2)
    def _(kb2):
        for kb in (2 * kb2, jnp.minimum(2 * kb2 + 1, i - 1)):
            for h in range(N_PAIRS):
                mx_ref[h] = jnp.maximum(mx_ref[h], scores(h, pl.multiple_of(kb * tq, tq)))

    for h in range(N_PAIRS):
        m2 = jnp.max(mx_ref[h], axis=-1, keepdims=True)
        mx_ref[h] = jnp.broadcast_to(m2, mx_ref.shape[1:])

    def weigh(h, kb):
        start = pl.multiple_of(kb * tq, tq)
        m2 = mx_ref[h]
        p = jnp.exp2(s_ref[h, :, pl.ds(start, tq)] - jnp.concatenate([m2, m2], axis=1))
        v_ones = jnp.concatenate([v_ref[pl.ds(start, tq), _pair(h)], ones], axis=1)
        return _dot(p.astype(BF16), v_ones)

    for h in range(N_PAIRS):
        acc_ref[h] = weigh(h, i)

    @pl.loop(0, i // 2)
    def _(kb2):
        for h in range(N_PAIRS):
            acc_ref[h] += weigh(h, 2 * kb2) + weigh(h, 2 * kb2 + 1)

    @pl.when(i % 2 == 1)
    def _():
        for h in range(N_PAIRS):
            acc_ref[h] += weigh(h, i - 1)

    lam = _lambda(lam_ref, lam_init)
    sw = sw_ref[...]
    for h in range(N_PAIRS):
        acc = acc_ref[h]
        o = acc[:tq, :PAIR] / acc[:tq, PAIR:] - lam * (acc[tq:, :PAIR] / acc[tq:, PAIR:])
        o_ref[:, _pair(h)] = _da_gate(o, sw, g_ref[:, _pair(h)], lam_init)


def _sb_prompt_kernel(q_ref, kt_ref, vt_ref, g_ref, o_ref, q2_ref, run_ref, acc_ref):
    i = pl.program_id(1)
    tq = q_ref.shape[0]
    row = lax.broadcasted_iota(jnp.int32, (2 * tq, tq), 0) % tq
    col = lax.broadcasted_iota(jnp.int32, (2 * tq, tq), 1)
    diag_mask = col < row
    neg_tri = _neg_tri(tq)

    def block(start, mask=None, live=tq):
        for h in range(N_PAIRS):
            if live == tq:
                pick = lambda ref: ref[h]
            else:
                pick = lambda ref: jnp.concatenate([ref[h, :live], ref[h, tq:tq + live]], axis=0)
            z = _dot(pick(q2_ref), kt_ref[_pair(h), pl.ds(start, tq)])
            run, acc = _sb_step(z, lambda a: _nt_dot(a, vt_ref[_pair(h), pl.ds(start, tq)]), neg_tri,
                                (pick(run_ref), pick(acc_ref)), mask)
            for ref, new in ((run_ref, run), (acc_ref, acc)):
                if live == tq:
                    ref[h] = new
                else:
                    ref[h, :live] = new[:live]
                    ref[h, tq:tq + live] = new[live:]

    def earlier_blocks(live, alive):
        def earlier_block(state):
            kb, _ = state
            block(pl.multiple_of(kb * tq, tq), live=live)
            return kb - 1, _sb_alive(run_ref[...])

        lax.while_loop(lambda state: (state[0] >= 0) & (state[1] > 0), earlier_block, (i - 1, alive))

    run_ref[...] = jnp.zeros(run_ref.shape, F32)
    acc_ref[...] = jnp.zeros(acc_ref.shape, F32)
    for h in range(N_PAIRS):
        q2_ref[h] = _split_pair(q_ref[:, _pair(h)])
    block(pl.multiple_of(i * tq, tq), diag_mask)

    late = SB_EARLY_ROWS * tq // ATTN_TILE
    worst = jnp.max(run_ref[...], axis=0)
    late_alive = jnp.maximum(_sb_alive(worst[late:tq]), _sb_alive(worst[tq + late:]))
    early_alive = jnp.maximum(_sb_alive(worst[:late]), _sb_alive(worst[tq:tq + late]))

    @pl.when(late_alive > 0)
    def _():
        earlier_blocks(tq, late_alive)

    @pl.when(late_alive == 0)
    def _():
        earlier_blocks(late, early_alive)

    for h in range(N_PAIRS):
        o_ref[:, _pair(h)] = _sb_finish((run_ref[h], acc_ref[h]), g_ref[:, _pair(h)])


def _prompt_attention(kernel_fn, name, batch, seq, q, k, v, g, feature_major_kv, scratch,
                      prefix_args=(), prefix_specs=()):
    tq = ATTN_TILE
    qspec = pl.BlockSpec((None, tq, WIDTH), lambda b, i: (b, i, 0))
    shape3 = lambda a: a.reshape(batch, seq, WIDTH)
    if feature_major_kv:
        kvspec = pl.BlockSpec((None, WIDTH, seq), lambda b, i: (b, 0, 0))
    else:
        kvspec = pl.BlockSpec((None, seq, WIDTH), lambda b, i: (b, 0, 0))
        k, v = shape3(k), shape3(v)
    out = pl.pallas_call(
        kernel_fn,
        grid=(batch, seq // tq),
        in_specs=list(prefix_specs) + [qspec, kvspec, kvspec, qspec],
        out_specs=qspec,
        out_shape=jax.ShapeDtypeStruct((batch, seq, WIDTH), BF16),
        scratch_shapes=scratch,
        compiler_params=pltpu.CompilerParams(dimension_semantics=("parallel", "arbitrary"),
                                             vmem_limit_bytes=VMEM_LIMIT),
        name=name,
    )(*prefix_args, shape3(q), k, v, shape3(g))
    return out.reshape(batch * seq, WIDTH)


def _pad_rows(a, rows):
    return jnp.concatenate([a, jnp.zeros((rows - a.shape[0], a.shape[1]), a.dtype)], axis=0)


def _da_decode_kernel(lam_ref, sw_ref, q_ref, ck_ref, cv_ref, nk_ref, nv_ref, g_ref, o_ref,
                      q_all_ref, m_ref, l_ref, acc_ref, *, lam_init, past_len):
    j = pl.program_id(1)
    t = q_ref.shape[0]
    n_rows = N_PAIRS * 2 * t
    q_head = lax.broadcasted_iota(jnp.int32, (n_rows, 1), 0) // (2 * t)

    @pl.when(j == 0)
    def _():
        m_ref[...] = jnp.full(m_ref.shape, NEG_INF, F32)
        l_ref[...] = jnp.zeros(l_ref.shape, F32)
        acc_ref[...] = jnp.zeros(acc_ref.shape, F32)
        q_all_ref[...] = jnp.concatenate([_split_pair(q_ref[:, _pair(h)]) for h in range(N_PAIRS)], axis=0)

    key_head = lax.broadcasted_iota(jnp.int32, (1, ck_ref.shape[0]), 1) % N_PAIRS
    s = jnp.where(key_head == q_head, _nt_dot(q_all_ref[...], ck_ref[...].astype(BF16)), NEG_INF)
    m, l, acc = _softmax_step(s, cv_ref[...].astype(BF16), (m_ref[...], l_ref[...], acc_ref[...]))
    m_ref[...] = m
    l_ref[...] = l
    acc_ref[...] = acc

    @pl.when(j == pl.num_programs(1) - 1)
    def _():
        lam = _lambda(lam_ref, lam_init)
        sw = sw_ref[...]
        stack = lambda ref: _pad_rows(jnp.concatenate([ref[:, _pair(h)] for h in range(N_PAIRS)], axis=0), PAIR)
        k_idx = lax.broadcasted_iota(jnp.int32, (1, PAIR), 1)
        q_pos = past_len + lax.broadcasted_iota(jnp.int32, (n_rows, 1), 0) % t
        mask = ((k_idx // t == q_head) & (((past_len + k_idx % t) // CHUNK) <= (q_pos // CHUNK)))
        s = jnp.where(mask, _nt_dot(q_all_ref[...], stack(nk_ref)), NEG_INF)
        m, l, acc = _softmax_step(s, stack(nv_ref), (m_ref[...], l_ref[...], acc_ref[...]))
        for h in range(N_PAIRS):
            rows = slice(h * 2 * t, (h + 1) * 2 * t)
            o_ref[:, _pair(h)] = _da_finish((m[rows], l[rows], acc[rows]), lam, sw, g_ref[:, _pair(h)], lam_init)


def _sb_decode_kernel(q_ref, ckt_ref, cvt_ref, g_ref, in_a_ref, in_b_ref, o_ref, run_out_ref, acc_out_ref,
                      run_ref, acc_ref, *, first):
    j = pl.program_id(1)
    t = q_ref.shape[0]
    sub = ATTN_TILE
    n_sub = ckt_ref.shape[1] // sub
    neg_tri = _neg_tri(sub)

    @pl.when(j == 0)
    def _():
        if first:
            q_idx = lax.broadcasted_iota(jnp.int32, (2 * t, PAIR), 0) % t
            k_idx = lax.broadcasted_iota(jnp.int32, (2 * t, PAIR), 1)
            mask = k_idx < q_idx
            for h in range(N_PAIRS):
                q2 = _split_pair(q_ref[:, _pair(h)])
                nv = _pad_rows(in_b_ref[:, _pair(h)], PAIR)
                init = (jnp.zeros((2 * t, 1), F32), jnp.zeros((2 * t, PAIR), F32))
                run, acc = _sb_step(_nt_dot(q2, _pad_rows(in_a_ref[:, _pair(h)], PAIR)), lambda a: _dot(a, nv),
                                    neg_tri[:PAIR, :PAIR], init, mask)
                run_ref[h] = run
                acc_ref[h] = acc
        else:
            run_ref[...] = in_a_ref[...]
            acc_ref[...] = in_b_ref[...]

    for c in reversed(range(n_sub)):
        cols = slice(c * sub, (c + 1) * sub)

        @pl.when(_sb_alive(run_ref[...]) > 0)
        def _():
            for h in range(N_PAIRS):
                q2 = _split_pair(q_ref[:, _pair(h)])
                z = _dot(q2, ckt_ref[_pair(h), cols].astype(BF16))
                run, acc = _sb_step(z, lambda a: _nt_dot(a, cvt_ref[_pair(h), cols].astype(BF16)), neg_tri,
                                    (run_ref[h], acc_ref[h]))
                run_ref[h] = run
                acc_ref[h] = acc

    @pl.when(j == pl.num_programs(1) - 1)
    def _():
        run_out_ref[...] = run_ref[...]
        acc_out_ref[...] = acc_ref[...]
        for h in range(N_PAIRS):
            o_ref[:, _pair(h)] = _sb_finish((run_ref[h], acc_ref[h]), g_ref[:, _pair(h)])


def _sb_decode(layer, t, q, cache_kt, cache_vt, nk, nv, g):
    batch, past = cache_kt.shape[1], cache_kt.shape[3]
    n_blocks = past // CACHE_TILE
    tok = pl.BlockSpec((None, t, WIDTH), lambda b, j: (b, 0, 0))
    run_spec = pl.BlockSpec((None, N_PAIRS, 2 * t, 1), lambda b, j: (b, 0, 0, 0))
    acc_spec = pl.BlockSpec((None, N_PAIRS, 2 * t, PAIR), lambda b, j: (b, 0, 0, 0))
    run_shape = jax.ShapeDtypeStruct((batch, N_PAIRS, 2 * t, 1), F32)
    acc_shape = jax.ShapeDtypeStruct((batch, N_PAIRS, 2 * t, PAIR), F32)
    shape3 = lambda a: a.reshape(batch, t, WIDTH)

    def call(first, steps, newest_block, in_a, in_b, in_spec_a, in_spec_b):
        cache = pl.BlockSpec((None, None, WIDTH, CACHE_TILE), lambda b, j: (layer, b, 0, newest_block - j))
        return pl.pallas_call(
            functools.partial(_sb_decode_kernel, first=first),
            grid=(batch, steps),
            in_specs=[tok, cache, cache, tok, in_spec_a, in_spec_b],
            out_specs=[tok, run_spec, acc_spec],
            out_shape=[jax.ShapeDtypeStruct((batch, t, WIDTH), BF16), run_shape, acc_shape],
            scratch_shapes=[pltpu.VMEM((N_PAIRS, 2 * t, 1), F32), pltpu.VMEM((N_PAIRS, 2 * t, PAIR), F32)],
            compiler_params=pltpu.CompilerParams(dimension_semantics=("parallel", "arbitrary"),
                                                 vmem_limit_bytes=VMEM_LIMIT),
            name="sb_decode",
        )(shape3(q), cache_kt, cache_vt, shape3(g), in_a, in_b)

    y, run, acc = call(True, 1, n_blocks - 1, shape3(nk), shape3(nv), tok, tok)
    if n_blocks > 1:
        y = lax.cond(jnp.max(run) > DEAD_LOG,
                     lambda: call(False, n_blocks - 1, n_blocks - 2, run, acc, run_spec, acc_spec)[0],
                     lambda: y)
    return y.reshape(batch * t, WIDTH)


def _decode_attention(kernel_fn, name, t, q, cache_k, cache_v, cache_spec, nk, nv, g, scratch,
                      prefix_args=(), prefix_specs=()):
    batch = cache_k.shape[1]
    tok = pl.BlockSpec((None, t, WIDTH), lambda b, j: (b, 0, 0))
    shape3 = lambda a: a.reshape(batch, t, WIDTH)
    out = pl.pallas_call(
        kernel_fn,
        grid=(batch, cache_k.size // (cache_k.shape[0] * batch * WIDTH * CACHE_TILE)),
        in_specs=list(prefix_specs) + [tok, cache_spec, cache_spec, tok, tok, tok],
        out_specs=tok,
        out_shape=jax.ShapeDtypeStruct((batch, t, WIDTH), BF16),
        scratch_shapes=scratch,
        compiler_params=pltpu.CompilerParams(dimension_semantics=("parallel", "arbitrary"),
                                             vmem_limit_bytes=VMEM_LIMIT),
        name=name,
    )(*prefix_args, shape3(q), cache_k, cache_v, shape3(nk), shape3(nv), shape3(g))
    return out.reshape(batch * t, WIDTH)


def _rope_tables(pos):
    half = HEAD_DIM // 2
    inv = ROPE_THETA ** (-jnp.arange(half, dtype=F32) / half)
    ang = pos.astype(F32)[:, None] * inv[None, :]
    cos = jnp.cos(ang)
    sin = jnp.sin(ang)
    reps = PAIR // HEAD_DIM
    return (jnp.tile(jnp.concatenate([cos, cos], axis=1), (1, reps)),
            jnp.tile(jnp.concatenate([-sin, sin], axis=1), (1, reps)))


def kernel(x_prompt, x_sample, cache_da_k, cache_da_v, cache_sb_k, cache_sb_v, norm_w, w_in, lambda_qk,
           subln_w, w_proj_da, w_proj_sb, w_out, final_norm_w):
    batch, seq, d = x_prompt.shape
    dec_batch, dec_seq, _ = x_sample.shape
    depth, _, past_len, da_heads, _ = cache_da_k.shape
    sb_heads = cache_sb_k.shape[3]
    assert seq % ATTN_TILE == 0 and ATTN_TILE % CHUNK == 0 and seq % TOKEN_TILE == 0
    assert dec_batch * dec_seq <= TOKEN_TILE and past_len % CACHE_TILE == 0 and N_PAIRS * dec_seq <= PAIR
    assert da_heads == N_PAIRS and cache_da_k.shape[4] == PAIR and sb_heads * HEAD_DIM == WIDTH

    cos_p, sin_p = _rope_tables(jnp.arange(seq, dtype=jnp.int32))
    cos_s, sin_s = _rope_tables(jnp.tile(past_len + jnp.arange(dec_seq, dtype=jnp.int32), dec_batch))

    w_in_bf = w_in.astype(BF16)
    wkv_t_bf = jnp.swapaxes(w_in[:, :, G_KS * WIDTH:(G_VS + 1) * WIDTH], 1, 2).astype(BF16)
    wda_bf = w_proj_da.astype(BF16)
    wsb_bf = w_proj_sb.astype(BF16)
    wout_bf = w_out.astype(BF16)
    norm_w3 = norm_w.reshape(depth, 1, d)
    subln_w3 = subln_w.reshape(depth, 1, PAIR)
    fnw = final_norm_w.reshape(1, d)
    feature_major = lambda c: jnp.transpose(c, (0, 1, 3, 4, 2)).reshape(depth, dec_batch, WIDTH, past_len)
    cskt, csvt = feature_major(cache_sb_k), feature_major(cache_sb_v)
    head_rows = lambda c: c.reshape(depth, dec_batch, past_len * da_heads, PAIR)
    cdk, cdv = head_rows(cache_da_k), head_rows(cache_da_v)

    xp = x_prompt.reshape(batch * seq, d)
    xs = x_sample.reshape(dec_batch * dec_seq, d)
    kv_stack, rows_s = None, []
    for layer in range(depth):
        lam_init = 0.8 - 0.6 * math.exp(-0.3 * layer)
        lam_args = (lambda_qk, subln_w3)
        lam_specs = (pl.BlockSpec((None,) + lambda_qk.shape[1:], lambda b, i, layer=layer: (layer, 0, 0)),
                     pl.BlockSpec((None, 1, PAIR), lambda b, i, layer=layer: (layer, 0, 0)))
        last = layer == depth - 1

        (qd, kd, vd, gd, qs, kst, vst, gs, kdb, vdb, kstb, vstb) = _inproj(
            xp, norm_w3, w_in_bf, layer, cos_p, sin_p, wkv_t_bf, kv_stack)
        kv_stack = (kd, vd, kst, vst)
        rows2 = (N_PAIRS, 2 * ATTN_TILE)
        yda = _prompt_attention(
            functools.partial(_da_prompt_kernel, lam_init=lam_init), "da_prompt",
            batch, seq, qd, kdb, vdb, gd, False,
            [pltpu.VMEM(rows2 + (PAIR,), BF16), pltpu.VMEM(rows2 + (seq,), F32),
             pltpu.VMEM(rows2 + (PAIR,), F32), pltpu.VMEM(rows2 + (2 * PAIR,), F32)],
            lam_args, lam_specs)
        ysb = _prompt_attention(
            _sb_prompt_kernel, "sb_prompt", batch, seq, qs, kstb, vstb, gs, True,
            [pltpu.VMEM(rows2 + (PAIR,), BF16), pltpu.VMEM(rows2 + (1,), F32),
             pltpu.VMEM(rows2 + (PAIR,), F32)])
        xp = _post(xp, norm_w3, w_in_bf, yda, ysb, wda_bf, wsb_bf, wout_bf, fnw, layer, last)

        (qd, kd, vd, gd, qs, ks, vs, gs, kdb, vdb, ksb, vsb) = _inproj(
            xs, norm_w3, w_in_bf, layer, cos_s, sin_s)
        all_rows = N_PAIRS * 2 * dec_seq
        yda = _decode_attention(
            functools.partial(_da_decode_kernel, lam_init=lam_init, past_len=past_len), "da_decode",
            dec_seq, qd, cdk, cdv,
            pl.BlockSpec((None, None, CACHE_TILE * da_heads, PAIR), lambda b, j, layer=layer: (layer, b, j, 0)),
            kdb, vdb, gd,
            [pltpu.VMEM((all_rows, PAIR), BF16), pltpu.VMEM((all_rows, 1), F32),
             pltpu.VMEM((all_rows, 1), F32), pltpu.VMEM((all_rows, PAIR), F32)],
            lam_args, lam_specs)
        ysb = _sb_decode(layer, dec_seq, qs, cskt, csvt, ksb, vsb, gs)
        xs = _post(xs, norm_w3, w_in_bf, yda, ysb, wda_bf, wsb_bf, wout_bf, fnw, layer, last)
        rows_s.append((kd, vd, ks, vs))

    def stack(rows, idx, shape):
        return jnp.stack([r[idx] for r in rows], axis=0).reshape((depth,) + shape)

    def token_major(a):
        return jnp.transpose(a.reshape(depth, batch, sb_heads, HEAD_DIM, seq), (0, 1, 4, 2, 3))

    sb_s = (dec_batch, dec_seq, sb_heads, HEAD_DIM)
    da_s = (dec_batch, dec_seq, da_heads, PAIR)
    kd, vd, kst, vst = kv_stack
    return (xp.reshape(batch, seq, d), xs.reshape(dec_batch, dec_seq, d),
            kd.reshape(depth, batch, seq, da_heads, PAIR), vd.reshape(depth, batch, seq, da_heads, PAIR),
            token_major(kst), token_major(vst),
            stack(rows_s, 0, da_s), stack(rows_s, 1, da_s), stack(rows_s, 2, sb_s), stack(rows_s, 3, sb_s))
```

```python
import functools
import math

import jax
import jax.numpy as jnp
import numpy as np
from jax import lax
from jax.experimental import pallas as pl
from jax.experimental.pallas import tpu as pltpu

F32 = jnp.float32
BF16 = jnp.bfloat16

CHUNK = 64
HEAD_DIM = 64
PAIR = 2 * HEAD_DIM
N_PAIRS = 4
WIDTH = N_PAIRS * PAIR
ROPE_THETA = 10000.0
EPS = 1e-6
NEG_INF = -1e30
QK_SCALE = HEAD_DIM ** -0.5
LOG2E = math.log2(math.e)
SIGN_BIT = np.uint32(0x80000000)
DEAD_LOG = -150.0 * math.log(2.0)

TOKEN_TILE = 512
ATTN_TILE = 256
SB_EARLY_ROWS = 192
CACHE_TILE = 1024
VMEM_LIMIT = 56 * 1024 * 1024

G_QD, G_KD, G_VD, G_GD, G_QS, G_KS, G_VS, G_GS, G_GA = 0, 1, 2, 3, 4, 5, 6, 7, 8
KV_OUTPUTS = (1, 2, 5, 6)


def _sigmoid(x):
    return 1.0 / (1.0 + jnp.exp(-x))


def _nt_dot(a, b):
    return lax.dot_general(a, b, (((1,), (1,)), ((), ())), preferred_element_type=F32)


def _dot(a, b):
    return jnp.dot(a, b, preferred_element_type=F32)


def _inproj_kernel(*refs, feature_major_sb, n_aliased, zero_slots_except):
    if feature_major_sb:
        x_ref, nw_ref, w_ref, cos_ref, sin_ref, wkv_ref = refs[:6]
        outs = refs[6 + n_aliased:]
    else:
        x_ref, nw_ref, w_ref, cos_ref, sin_ref = refs[:5]
        outs = refs[5 + n_aliased:]
    qd_ref, kd_ref, vd_ref, gd_ref, qs_ref, ks_ref, vs_ref, gs_ref, kdb_ref, vdb_ref, ksb_ref, vsb_ref = outs
    if zero_slots_except is not None:
        stacked = (kd_ref, vd_ref, ks_ref, vs_ref)
        for ref in stacked:
            for slot in range(ref.shape[0]):
                if slot != zero_slots_except:
                    ref[slot] = jnp.zeros(ref.shape[1:], ref.dtype)
        kd_ref, vd_ref, ks_ref, vs_ref = (ref.at[zero_slots_except] for ref in stacked)

    x = x_ref[...]
    ms = jnp.mean(x * x, axis=-1, keepdims=True)
    h = (x * lax.rsqrt(ms + EPS) * nw_ref[...]).astype(BF16)

    def proj(group):
        return _dot(h, w_ref[:, group * WIDTH:(group + 1) * WIDTH])

    lane = lax.broadcasted_iota(jnp.int32, (1, PAIR), 1)
    upper_half = (lane & (HEAD_DIM // 2)) != 0
    cos = cos_ref[...]
    sin = sin_ref[...]

    def rope(s):
        partner = jnp.where(upper_half, pltpu.roll(s, HEAD_DIM // 2, 1),
                            pltpu.roll(s, PAIR - HEAD_DIM // 2, 1))
        return s * cos + partner * sin

    def slab(u, c):
        return u[:, c * PAIR:(c + 1) * PAIR]

    u = proj(G_QD)
    for c in range(N_PAIRS):
        qd_ref[:, c * PAIR:(c + 1) * PAIR] = (rope(slab(u, c)) * QK_SCALE).astype(BF16)
    tokens = x.shape[0]
    u = proj(G_KD)
    for c in range(N_PAIRS):
        kd = rope(slab(u, c))
        kd_ref[pl.ds(c, tokens, stride=N_PAIRS), :] = kd
        kdb_ref[:, c * PAIR:(c + 1) * PAIR] = kd.astype(BF16)
    u = proj(G_VD)
    for c in range(N_PAIRS):
        vd_ref[pl.ds(c, tokens, stride=N_PAIRS), :] = slab(u, c)
    vdb_ref[...] = u.astype(BF16)
    g = proj(G_GD)
    gd_ref[...] = (g * _sigmoid(g)).astype(BF16)
    qs_ref[...] = (proj(G_QS) * QK_SCALE).astype(BF16)
    if feature_major_sb:
        for half, (o_ref, ob_ref) in enumerate(((ks_ref, ksb_ref), (vs_ref, vsb_ref))):
            ut = _nt_dot(wkv_ref[half * WIDTH:(half + 1) * WIDTH, :], h)
            o_ref[...] = ut
            ob_ref[...] = ut.astype(BF16)
    else:
        for group, o_ref, ob_ref in ((G_KS, ks_ref, ksb_ref), (G_VS, vs_ref, vsb_ref)):
            u = proj(group)
            o_ref[...] = u
            ob_ref[...] = u.astype(BF16)
    g = proj(G_GS)
    gs_ref[...] = (g * _sigmoid(g)).astype(BF16)


def _inproj(x, norm_w, w_in_bf, layer, cos_tab, sin_tab, wkv_t_bf=None, kv_stack=None):
    n, d = x.shape
    creates_stack = wkv_t_bf is not None and kv_stack is None
    tm = min(TOKEN_TILE // 2 if creates_stack else TOKEN_TILE, n)
    seq = cos_tab.shape[0]
    n_tab = seq // tm
    depth, _, cols = w_in_bf.shape
    row = lambda i: (i, 0)
    wide = pl.BlockSpec((tm, WIDTH), row)
    full = pl.BlockSpec((tm, d), row)
    tab = pl.BlockSpec((tm, PAIR), lambda i: (i % n_tab, 0))
    weight = lambda r, c: pl.BlockSpec((None, r, c), lambda i: (layer, 0, 0), pipeline_mode=pl.Buffered(1))
    bf_out = jax.ShapeDtypeStruct((n, WIDTH), BF16)
    in_specs = [full, pl.BlockSpec((None, 1, d), lambda i: (layer, 0, 0)), weight(d, G_GA * WIDTH), tab, tab]
    args = [x, norm_w, w_in_bf, cos_tab, sin_tab]
    aliases = {}
    if wkv_t_bf is not None:
        in_specs.append(weight(2 * WIDTH, d))
        args.append(wkv_t_bf)
        slots, slot = (depth, 0) if creates_stack else (None, layer)
        heads = pl.BlockSpec((slots, tm * N_PAIRS, PAIR), lambda i: (slot, i, 0))
        f32_heads = jax.ShapeDtypeStruct((depth, n * N_PAIRS, PAIR), F32)
        sb_spec = pl.BlockSpec((slots, None, WIDTH, tm), lambda i: (slot, i // n_tab, 0, i % n_tab))
        sb_f32 = jax.ShapeDtypeStruct((depth, n // seq, WIDTH, seq), F32)
        sb_bf_spec = pl.BlockSpec((None, WIDTH, tm), lambda i: (i // n_tab, 0, i % n_tab))
        sb_bf = jax.ShapeDtypeStruct((n // seq, WIDTH, seq), BF16)
        if kv_stack is not None:
            for out_index, a in zip(KV_OUTPUTS, kv_stack):
                aliases[len(args)] = out_index
                in_specs.append(pl.BlockSpec(memory_space=pl.ANY))
                args.append(a)
    else:
        heads = pl.BlockSpec((tm * N_PAIRS, PAIR), lambda i: (i, 0))
        f32_heads = jax.ShapeDtypeStruct((n * N_PAIRS, PAIR), F32)
        sb_spec = sb_bf_spec = wide
        sb_f32 = jax.ShapeDtypeStruct((n, WIDTH), F32)
        sb_bf = bf_out
    return pl.pallas_call(
        functools.partial(_inproj_kernel, feature_major_sb=wkv_t_bf is not None, n_aliased=len(aliases),
                          zero_slots_except=layer if creates_stack else None),
        grid=(n // tm,),
        in_specs=in_specs,
        out_specs=[wide, heads, heads, wide, wide, sb_spec, sb_spec, wide, wide, wide, sb_bf_spec, sb_bf_spec],
        out_shape=[bf_out, f32_heads, f32_heads, bf_out, bf_out, sb_f32, sb_f32, bf_out,
                   bf_out, bf_out, sb_bf, sb_bf],
        input_output_aliases=aliases,
        compiler_params=pltpu.CompilerParams(dimension_semantics=("parallel",),
                                             vmem_limit_bytes=VMEM_LIMIT),
        name="inproj",
    )(*args)


def _post_kernel(x_ref, nw_ref, wg_ref, yda_ref, ysb_ref, wda_ref, wsb_ref, wout_ref, fnw_ref, o_ref,
                 *, final_norm):
    x = x_ref[...]
    d = x.shape[1]
    ms = jnp.mean(x * x, axis=-1, keepdims=True)
    h = (x * lax.rsqrt(ms + EPS) * nw_ref[...]).astype(BF16)
    merged = (_sigmoid(_dot(h, wg_ref[:, :d])) * _dot(yda_ref[...], wda_ref[...])
              + _sigmoid(_dot(h, wg_ref[:, d:])) * _dot(ysb_ref[...], wsb_ref[...])).astype(BF16)
    out = x + _dot(merged, wout_ref[...])
    if final_norm:
        ms = jnp.mean(out * out, axis=-1, keepdims=True)
        out = out * lax.rsqrt(ms + EPS) * fnw_ref[...]
    o_ref[...] = out


def _post(x, norm_w, w_in_bf, yda, ysb, wda_bf, wsb_bf, wout_bf, final_norm_w, layer, final_norm):
    n, d = x.shape
    tm = min(TOKEN_TILE, n)
    assert G_GA * WIDTH % (2 * d) == 0 and (G_GA * WIDTH + 2 * d) == w_in_bf.shape[2]
    row = lambda i: (i, 0)
    wide = pl.BlockSpec((tm, WIDTH), row)
    full = pl.BlockSpec((tm, d), row)
    weight = lambda r: pl.BlockSpec((None, r, d), lambda i: (layer, 0, 0), pipeline_mode=pl.Buffered(1))
    gate_weight = pl.BlockSpec((None, d, 2 * d), lambda i: (layer, 0, G_GA * WIDTH // (2 * d)),
                               pipeline_mode=pl.Buffered(1))
    return pl.pallas_call(
        functools.partial(_post_kernel, final_norm=final_norm),
        grid=(n // tm,),
        in_specs=[full, pl.BlockSpec((None, 1, d), lambda i: (layer, 0, 0)), gate_weight, wide, wide,
                  weight(WIDTH), weight(WIDTH), weight(d), pl.BlockSpec((1, d), lambda i: (0, 0))],
        out_specs=full,
        out_shape=jax.ShapeDtypeStruct((n, d), F32),
        compiler_params=pltpu.CompilerParams(dimension_semantics=("parallel",),
                                             vmem_limit_bytes=VMEM_LIMIT),
        name="post",
    )(x, norm_w, w_in_bf, yda, ysb, wda_bf, wsb_bf, wout_bf, final_norm_w)


def _split_pair(q):
    lane = lax.broadcasted_iota(jnp.int32, q.shape, 1)
    first = lane < HEAD_DIM
    zero = jnp.zeros_like(q)
    return jnp.concatenate([jnp.where(first, q, zero), jnp.where(first, zero, q)], axis=0)


def _lambda(lam_ref, lam_init):
    lf = lam_ref[...]
    a = jnp.sum(lf[0:1] * lf[1:2], axis=-1, keepdims=True)
    b = jnp.sum(lf[2:3] * lf[3:4], axis=-1, keepdims=True)
    return jnp.exp(a) - jnp.exp(b) + lam_init


def _softmax_step(s, v, carry):
    m, l, acc = carry
    m_new = jnp.maximum(m, jnp.max(s, axis=-1, keepdims=True))
    alpha = jnp.exp(m - m_new)
    p = jnp.exp(s - m_new)
    l = alpha * l + jnp.sum(p, axis=-1, keepdims=True)
    acc = alpha * acc + _dot(p.astype(BF16), v)
    return m_new, l, acc


def _da_finish(carry, lam, sw, gate, lam_init):
    _, l, acc = carry
    t = acc.shape[0] // 2
    o = acc[:t] * (1.0 / l[:t]) - lam * (acc[t:] * (1.0 / l[t:]))
    return _da_gate(o, sw, gate, lam_init)


def _da_gate(o, sw, gate, lam_init):
    ms = jnp.mean(o * o, axis=-1, keepdims=True)
    y = (o * lax.rsqrt(ms + EPS) * sw) * (1.0 - lam_init)
    return (y * gate.astype(F32)).astype(BF16)


def _neg_tri(n):
    j = lax.broadcasted_iota(jnp.int32, (n, n), 0)
    s = lax.broadcasted_iota(jnp.int32, (n, n), 1)
    return jnp.where(j >= s, -1.0, 0.0).astype(BF16)


def _sb_step(z, pv, neg_tri, carry, mask=None):
    run, acc = carry
    hi, lo, total = _sb_front(z, mask)
    c = _dot(hi, neg_tri) + _dot(lo, neg_tri)
    return run - total, acc + pv(_sb_back(z, c, run, mask))


def _sb_front(z, mask=None):
    neg_abs = lax.bitcast_convert_type(lax.bitcast_convert_type(z * LOG2E, jnp.uint32) | SIGN_BIT, F32)
    sp = jnp.maximum(z, 0.0) + jnp.log(1.0 + jnp.exp2(neg_abs))
    if mask is not None:
        sp = jnp.where(mask, sp, 0.0)
    hi = sp.astype(BF16)
    lo = (sp - hi.astype(F32)).astype(BF16)
    return hi, lo, jnp.sum(sp, axis=-1, keepdims=True)


def _sb_back(z, c, run, mask=None):
    if run.shape[1] != 1:
        run = jnp.concatenate([run] * (z.shape[1] // run.shape[1]), axis=1)
    a = jnp.exp2((z + c + run) * LOG2E)
    if mask is not None:
        a = jnp.where(mask, a, 0.0)
    return a.astype(BF16)


def _sb_alive(run):
    return (jnp.max(run) > DEAD_LOG).astype(jnp.int32)


def _sb_finish(carry, gate):
    _, acc = carry
    t = acc.shape[0] // 2
    lane = lax.broadcasted_iota(jnp.int32, (t, PAIR), 1)
    o = jnp.where(lane < HEAD_DIM, acc[:t], acc[t:])
    return (o * gate.astype(F32)).astype(BF16)


def _pair(h):
    return slice(h * PAIR, (h + 1) * PAIR)


def _da_prompt_kernel(lam_ref, sw_ref, q_ref, k_ref, v_ref, g_ref, o_ref, q2_ref, s_ref, mx_ref, acc_ref,
                      *, lam_init):
    i = pl.program_id(1)
    tq = q_ref.shape[0]
    row = lax.broadcasted_iota(jnp.int32, (2 * tq, tq), 0) % tq
    col = lax.broadcasted_iota(jnp.int32, (2 * tq, tq), 1)
    diag_mask = (col // CHUNK) <= (row // CHUNK)
    ones = jnp.ones((tq, PAIR), BF16)

    def scores(h, start, mask=None):
        s = _nt_dot(q2_ref[h], k_ref[pl.ds(start, tq), _pair(h)]) * LOG2E
        if mask is not None:
            s = jnp.where(mask, s, NEG_INF)
        s_ref[h, :, pl.ds(start, tq)] = s
        return jnp.maximum(s[:, :PAIR], s[:, PAIR:])

    for h in range(N_PAIRS):
        q2_ref[h] = _split_pair(q_ref[:, _pair(h)])
    for h in range(N_PAIRS):
        mx_ref[h] = scores(h, pl.multiple_of(i * tq, tq), diag_mask)

    @pl.loop(0, (i + 1) // 2)
    def _(kb2):
        for kb in (2 * kb2, jnp.minimum(2 * kb2 + 1, i - 1)):
            for h in range(N_PAIRS):
                mx_ref[h] = jnp.maximum(mx_ref[h], scores(h, pl.multiple_of(kb * tq, tq)))

    for h in range(N_PAIRS):
        m2 = jnp.max(mx_ref[h], axis=-1, keepdims=True)
        mx_ref[h] = jnp.broadcast_to(m2, mx_ref.shape[1:])

    def weigh(h, kb):
        start = pl.multiple_of(kb * tq, tq)
        m2 = mx_ref[h]
        p = jnp.exp2(s_ref[h, :, pl.ds(start, tq)] - jnp.concatenate([m2, m2], axis=1))
        v_ones = jnp.concatenate([v_ref[pl.ds(start, tq), _pair(h)], ones], axis=1)
        return _dot(p.astype(BF16), v_ones)

    for h in range(N_PAIRS):
        acc_ref[h] = weigh(h, i)

    @pl.loop(0, i // 2)
    def _(kb2):
        for h in range(N_PAIRS):
            acc_ref[h] += weigh(h, 2 * kb2) + weigh(h, 2 * kb2 + 1)

    @pl.when(i % 2 == 1)
    def _():
        for h in range(N_PAIRS):
            acc_ref[h] += weigh(h, i - 1)

    lam = _lambda(lam_ref, lam_init)
    sw = sw_ref[...]
    for h in range(N_PAIRS):
        acc = acc_ref[h]
        o = acc[:tq, :PAIR] / acc[:tq, PAIR:] - lam * (acc[tq:, :PAIR] / acc[tq:, PAIR:])
        o_ref[:, _pair(h)] = _da_gate(o, sw, g_ref[:, _pair(h)], lam_init)


def _sb_prompt_kernel(q_ref, kt_ref, vt_ref, g_ref, o_ref, q2_ref, run_ref, acc_ref):
    i = pl.program_id(1)
    tq = q_ref.shape[0]
    row = lax.broadcasted_iota(jnp.int32, (2 * tq, tq), 0) % tq
    col = lax.broadcasted_iota(jnp.int32, (2 * tq, tq), 1)
    diag_mask = col < row
    neg_tri = _neg_tri(tq)

    def block(start, mask=None, live=tq):
        for h in range(N_PAIRS):
            if live == tq:
                pick = lambda ref: ref[h]
            else:
                pick = lambda ref: jnp.concatenate([ref[h, :live], ref[h, tq:tq + live]], axis=0)
            z = _dot(pick(q2_ref), kt_ref[_pair(h), pl.ds(start, tq)])
            run, acc = _sb_step(z, lambda a: _nt_dot(a, vt_ref[_pair(h), pl.ds(start, tq)]), neg_tri,
                                (pick(run_ref), pick(acc_ref)), mask)
            for ref, new in ((run_ref, run), (acc_ref, acc)):
                if live == tq:
                    ref[h] = new
                else:
                    ref[h, :live] = new[:live]
                    ref[h, tq:tq + live] = new[live:]

    def earlier_blocks(live, alive):
        def earlier_block(state):
            kb, _ = state
            block(pl.multiple_of(kb * tq, tq), live=live)
            return kb - 1, _sb_alive(run_ref[...])

        lax.while_loop(lambda state: (state[0] >= 0) & (state[1] > 0), earlier_block, (i - 1, alive))

    run_ref[...] = jnp.zeros(run_ref.shape, F32)
    acc_ref[...] = jnp.zeros(acc_ref.shape, F32)
    for h in range(N_PAIRS):
        q2_ref[h] = _split_pair(q_ref[:, _pair(h)])
    block(pl.multiple_of(i * tq, tq), diag_mask)

    late = SB_EARLY_ROWS * tq // ATTN_TILE
    worst = jnp.max(run_ref[...], axis=0)
    late_alive = jnp.maximum(_sb_alive(worst[late:tq]), _sb_alive(worst[tq + late:]))
    early_alive = jnp.maximum(_sb_alive(worst[:late]), _sb_alive(worst[tq:tq + late]))

    @pl.when(late_alive > 0)
    def _():
        earlier_blocks(tq, late_alive)

    @pl.when(late_alive == 0)
    def _():
        earlier_blocks(late, early_alive)

    for h in range(N_PAIRS):
        o_ref[:, _pair(h)] = _sb_finish((run_ref[h], acc_ref[h]), g_ref[:, _pair(h)])


def _both_prompt_kernel(lam_ref, sw_ref, qd_ref, kd_ref, vd_ref, gd_ref, qs_ref, kt_ref, vt_ref, gs_ref,
                        oda_ref, osb_ref, q2_ref, s_ref, mx_ref, acc2_ref, run_ref, acc_ref, *, lam_init):
    _da_prompt_kernel(lam_ref, sw_ref, qd_ref, kd_ref, vd_ref, gd_ref, oda_ref, q2_ref, s_ref, mx_ref, acc2_ref,
                      lam_init=lam_init)
    _sb_prompt_kernel(qs_ref, kt_ref, vt_ref, gs_ref, osb_ref, q2_ref, run_ref, acc_ref)


def _both_prompt_attention(batch, seq, lam_init, lam_args, lam_specs, qd, kd, vd, gd, qs, kt, vt, gs):
    tq = ATTN_TILE
    rows2 = (N_PAIRS, 2 * tq)
    qspec = pl.BlockSpec((None, tq, WIDTH), lambda b, i: (b, i, 0))
    rowspec = pl.BlockSpec((None, seq, WIDTH), lambda b, i: (b, 0, 0))
    colspec = pl.BlockSpec((None, WIDTH, seq), lambda b, i: (b, 0, 0))
    shape3 = lambda a: a.reshape(batch, seq, WIDTH)
    out = jax.ShapeDtypeStruct((batch, seq, WIDTH), BF16)
    yda, ysb = pl.pallas_call(
        functools.partial(_both_prompt_kernel, lam_init=lam_init),
        grid=(batch, seq // tq),
        in_specs=list(lam_specs) + [qspec, rowspec, rowspec, qspec, qspec, colspec, colspec, qspec],
        out_specs=[qspec, qspec],
        out_shape=[out, out],
        scratch_shapes=[pltpu.VMEM(rows2 + (PAIR,), BF16), pltpu.VMEM(rows2 + (seq,), F32),
                        pltpu.VMEM(rows2 + (PAIR,), F32), pltpu.VMEM(rows2 + (2 * PAIR,), F32),
                        pltpu.VMEM(rows2 + (1,), F32), pltpu.VMEM(rows2 + (PAIR,), F32)],
        compiler_params=pltpu.CompilerParams(dimension_semantics=("parallel", "arbitrary"),
                                             vmem_limit_bytes=VMEM_LIMIT),
        name="both_prompt",
    )(*lam_args, shape3(qd), shape3(kd), shape3(vd), shape3(gd), shape3(qs), kt, vt, shape3(gs))
    return yda.reshape(batch * seq, WIDTH), ysb.reshape(batch * seq, WIDTH)


def _prompt_attention(kernel_fn, name, batch, seq, q, k, v, g, feature_major_kv, scratch,
                      prefix_args=(), prefix_specs=()):
    tq = ATTN_TILE
    qspec = pl.BlockSpec((None, tq, WIDTH), lambda b, i: (b, i, 0))
    shape3 = lambda a: a.reshape(batch, seq, WIDTH)
    if feature_major_kv:
        kvspec = pl.BlockSpec((None, WIDTH, seq), lambda b, i: (b, 0, 0))
    else:
        kvspec = pl.BlockSpec((None, seq, WIDTH), lambda b, i: (b, 0, 0))
        k, v = shape3(k), shape3(v)
    out = pl.pallas_call(
        kernel_fn,
        grid=(batch, seq // tq),
        in_specs=list(prefix_specs) + [qspec, kvspec, kvspec, qspec],
        out_specs=qspec,
        out_shape=jax.ShapeDtypeStruct((batch, seq, WIDTH), BF16),
        scratch_shapes=scratch,
        compiler_params=pltpu.CompilerParams(dimension_semantics=("parallel", "arbitrary"),
                                             vmem_limit_bytes=VMEM_LIMIT),
        name=name,
    )(*prefix_args, shape3(q), k, v, shape3(g))
    return out.reshape(batch * seq, WIDTH)


def _pad_rows(a, rows):
    return jnp.concatenate([a, jnp.zeros((rows - a.shape[0], a.shape[1]), a.dtype)], axis=0)


def _da_decode_kernel(lam_ref, sw_ref, q_ref, ck_ref, cv_ref, nk_ref, nv_ref, g_ref, o_ref,
                      q_all_ref, m_ref, l_ref, acc_ref, *, lam_init, past_len):
    j = pl.program_id(1)
    t = q_ref.shape[0]
    n_rows = N_PAIRS * 2 * t
    q_head = lax.broadcasted_iota(jnp.int32, (n_rows, 1), 0) // (2 * t)

    @pl.when(j == 0)
    def _():
        m_ref[...] = jnp.full(m_ref.shape, NEG_INF, F32)
        l_ref[...] = jnp.zeros(l_ref.shape, F32)
        acc_ref[...] = jnp.zeros(acc_ref.shape, F32)
        q_all_ref[...] = jnp.concatenate([_split_pair(q_ref[:, _pair(h)]) for h in range(N_PAIRS)], axis=0)

    key_head = lax.broadcasted_iota(jnp.int32, (1, ck_ref.shape[0]), 1) % N_PAIRS
    s = jnp.where(key_head == q_head, _nt_dot(q_all_ref[...], ck_ref[...].astype(BF16)), NEG_INF)
    m, l, acc = _softmax_step(s, cv_ref[...].astype(BF16), (m_ref[...], l_ref[...], acc_ref[...]))
    m_ref[...] = m
    l_ref[...] = l
    acc_ref[...] = acc

    @pl.when(j == pl.num_programs(1) - 1)
    def _():
        lam = _lambda(lam_ref, lam_init)
        sw = sw_ref[...]
        stack = lambda ref: _pad_rows(jnp.concatenate([ref[:, _pair(h)] for h in range(N_PAIRS)], axis=0), PAIR)
        k_idx = lax.broadcasted_iota(jnp.int32, (1, PAIR), 1)
        q_pos = past_len + lax.broadcasted_iota(jnp.int32, (n_rows, 1), 0) % t
        mask = ((k_idx // t == q_head) & (((past_len + k_idx % t) // CHUNK) <= (q_pos // CHUNK)))
        s = jnp.where(mask, _nt_dot(q_all_ref[...], stack(nk_ref)), NEG_INF)
        m, l, acc = _softmax_step(s, stack(nv_ref), (m_ref[...], l_ref[...], acc_ref[...]))
        for h in range(N_PAIRS):
            rows = slice(h * 2 * t, (h + 1) * 2 * t)
            o_ref[:, _pair(h)] = _da_finish((m[rows], l[rows], acc[rows]), lam, sw, g_ref[:, _pair(h)], lam_init)


def _sb_decode_kernel(q_ref, ckt_ref, cvt_ref, g_ref, in_a_ref, in_b_ref, o_ref, run_out_ref, acc_out_ref,
                      run_ref, acc_ref, *, first):
    j = pl.program_id(1)
    t = q_ref.shape[0]
    sub = ATTN_TILE
    n_sub = ckt_ref.shape[1] // sub
    neg_tri = _neg_tri(sub)

    @pl.when(j == 0)
    def _():
        if first:
            q_idx = lax.broadcasted_iota(jnp.int32, (2 * t, PAIR), 0) % t
            k_idx = lax.broadcasted_iota(jnp.int32, (2 * t, PAIR), 1)
            mask = k_idx < q_idx
            for h in range(N_PAIRS):
                q2 = _split_pair(q_ref[:, _pair(h)])
                nv = _pad_rows(in_b_ref[:, _pair(h)], PAIR)
                init = (jnp.zeros((2 * t, 1), F32), jnp.zeros((2 * t, PAIR), F32))
                run, acc = _sb_step(_nt_dot(q2, _pad_rows(in_a_ref[:, _pair(h)], PAIR)), lambda a: _dot(a, nv),
                                    neg_tri[:PAIR, :PAIR], init, mask)
                run_ref[h] = run
                acc_ref[h] = acc
        else:
            run_ref[...] = in_a_ref[...]
            acc_ref[...] = in_b_ref[...]

    for c in reversed(range(n_sub)):
        cols = slice(c * sub, (c + 1) * sub)

        @pl.when(_sb_alive(run_ref[...]) > 0)
        def _():
            for h in range(N_PAIRS):
                q2 = _split_pair(q_ref[:, _pair(h)])
                z = _dot(q2, ckt_ref[_pair(h), cols].astype(BF16))
                run, acc = _sb_step(z, lambda a: _nt_dot(a, cvt_ref[_pair(h), cols].astype(BF16)), neg_tri,
                                    (run_ref[h], acc_ref[h]))
                run_ref[h] = run
                acc_ref[h] = acc

    @pl.when(j == pl.num_programs(1) - 1)
    def _():
        run_out_ref[...] = run_ref[...]
        acc_out_ref[...] = acc_ref[...]
        for h in range(N_PAIRS):
            o_ref[:, _pair(h)] = _sb_finish((run_ref[h], acc_ref[h]), g_ref[:, _pair(h)])


def _sb_decode(layer, t, q, cache_kt, cache_vt, nk, nv, g):
    batch, past = cache_kt.shape[1], cache_kt.shape[3]
    n_blocks = past // CACHE_TILE
    tok = pl.BlockSpec((None, t, WIDTH), lambda b, j: (b, 0, 0))
    run_spec = pl.BlockSpec((None, N_PAIRS, 2 * t, 1), lambda b, j: (b, 0, 0, 0))
    acc_spec = pl.BlockSpec((None, N_PAIRS, 2 * t, PAIR), lambda b, j: (b, 0, 0, 0))
    run_shape = jax.ShapeDtypeStruct((batch, N_PAIRS, 2 * t, 1), F32)
    acc_shape = jax.ShapeDtypeStruct((batch, N_PAIRS, 2 * t, PAIR), F32)
    shape3 = lambda a: a.reshape(batch, t, WIDTH)

    def call(first, steps, newest_block, in_a, in_b, in_spec_a, in_spec_b):
        cache = pl.BlockSpec((None, None, WIDTH, CACHE_TILE), lambda b, j: (layer, b, 0, newest_block - j))
        return pl.pallas_call(
            functools.partial(_sb_decode_kernel, first=first),
            grid=(batch, steps),
            in_specs=[tok, cache, cache, tok, in_spec_a, in_spec_b],
            out_specs=[tok, run_spec, acc_spec],
            out_shape=[jax.ShapeDtypeStruct((batch, t, WIDTH), BF16), run_shape, acc_shape],
            scratch_shapes=[pltpu.VMEM((N_PAIRS, 2 * t, 1), F32), pltpu.VMEM((N_PAIRS, 2 * t, PAIR), F32)],
            compiler_params=pltpu.CompilerParams(dimension_semantics=("parallel", "arbitrary"),
                                                 vmem_limit_bytes=VMEM_LIMIT),
            name="sb_decode",
        )(shape3(q), cache_kt, cache_vt, shape3(g), in_a, in_b)

    y, run, acc = call(True, 1, n_blocks - 1, shape3(nk), shape3(nv), tok, tok)
    if n_blocks > 1:
        y = lax.cond(jnp.max(run) > DEAD_LOG,
                     lambda: call(False, n_blocks - 1, n_blocks - 2, run, acc, run_spec, acc_spec)[0],
                     lambda: y)
    return y.reshape(batch * t, WIDTH)


def _decode_attention(kernel_fn, name, t, q, cache_k, cache_v, cache_spec, nk, nv, g, scratch,
                      prefix_args=(), prefix_specs=()):
    batch = cache_k.shape[1]
    tok = pl.BlockSpec((None, t, WIDTH), lambda b, j: (b, 0, 0))
    shape3 = lambda a: a.reshape(batch, t, WIDTH)
    out = pl.pallas_call(
        kernel_fn,
        grid=(batch, cache_k.size // (cache_k.shape[0] * batch * WIDTH * CACHE_TILE)),
        in_specs=list(prefix_specs) + [tok, cache_spec, cache_spec, tok, tok, tok],
        out_specs=tok,
        out_shape=jax.ShapeDtypeStruct((batch, t, WIDTH), BF16),
        scratch_shapes=scratch,
        compiler_params=pltpu.CompilerParams(dimension_semantics=("parallel", "arbitrary"),
                                             vmem_limit_bytes=VMEM_LIMIT),
        name=name,
    )(*prefix_args, shape3(q), cache_k, cache_v, shape3(nk), shape3(nv), shape3(g))
    return out.reshape(batch * t, WIDTH)


def _rope_tables(pos):
    half = HEAD_DIM // 2
    inv = ROPE_THETA ** (-jnp.arange(half, dtype=F32) / half)
    ang = pos.astype(F32)[:, None] * inv[None, :]
    cos = jnp.cos(ang)
    sin = jnp.sin(ang)
    reps = PAIR // HEAD_DIM
    return (jnp.tile(jnp.concatenate([cos, cos], axis=1), (1, reps)),
            jnp.tile(jnp.concatenate([-sin, sin], axis=1), (1, reps)))


def kernel(x_prompt, x_sample, cache_da_k, cache_da_v, cache_sb_k, cache_sb_v, norm_w, w_in, lambda_qk,
           subln_w, w_proj_da, w_proj_sb, w_out, final_norm_w):
    batch, seq, d = x_prompt.shape
    dec_batch, dec_seq, _ = x_sample.shape
    depth, _, past_len, da_heads, _ = cache_da_k.shape
    sb_heads = cache_sb_k.shape[3]
    assert seq % ATTN_TILE == 0 and ATTN_TILE % CHUNK == 0 and seq % TOKEN_TILE == 0
    assert dec_batch * dec_seq <= TOKEN_TILE and past_len % CACHE_TILE == 0 and N_PAIRS * dec_seq <= PAIR
    assert da_heads == N_PAIRS and cache_da_k.shape[4] == PAIR and sb_heads * HEAD_DIM == WIDTH

    cos_p, sin_p = _rope_tables(jnp.arange(seq, dtype=jnp.int32))
    cos_s, sin_s = _rope_tables(jnp.tile(past_len + jnp.arange(dec_seq, dtype=jnp.int32), dec_batch))

    w_in_bf = w_in.astype(BF16)
    wkv_t_bf = jnp.swapaxes(w_in[:, :, G_KS * WIDTH:(G_VS + 1) * WIDTH], 1, 2).astype(BF16)
    wda_bf = w_proj_da.astype(BF16)
    wsb_bf = w_proj_sb.astype(BF16)
    wout_bf = w_out.astype(BF16)
    norm_w3 = norm_w.reshape(depth, 1, d)
    subln_w3 = subln_w.reshape(depth, 1, PAIR)
    fnw = final_norm_w.reshape(1, d)
    feature_major = lambda c: jnp.transpose(c, (0, 1, 3, 4, 2)).reshape(depth, dec_batch, WIDTH, past_len)
    cskt, csvt = feature_major(cache_sb_k), feature_major(cache_sb_v)
    head_rows = lambda c: c.reshape(depth, dec_batch, past_len * da_heads, PAIR)
    cdk, cdv = head_rows(cache_da_k), head_rows(cache_da_v)

    xp = x_prompt.reshape(batch * seq, d)
    xs = x_sample.reshape(dec_batch * dec_seq, d)
    kv_stack, rows_s = None, []
    for layer in range(depth):
        lam_init = 0.8 - 0.6 * math.exp(-0.3 * layer)
        lam_args = (lambda_qk, subln_w3)
        lam_specs = (pl.BlockSpec((None,) + lambda_qk.shape[1:], lambda b, i, layer=layer: (layer, 0, 0)),
                     pl.BlockSpec((None, 1, PAIR), lambda b, i, layer=layer: (layer, 0, 0)))
        last = layer == depth - 1

        (qd, kd, vd, gd, qs, kst, vst, gs, kdb, vdb, kstb, vstb) = _inproj(
            xp, norm_w3, w_in_bf, layer, cos_p, sin_p, wkv_t_bf, kv_stack)
        kv_stack = (kd, vd, kst, vst)
        yda, ysb = _both_prompt_attention(batch, seq, lam_init, lam_args, lam_specs,
                                          qd, kdb, vdb, gd, qs, kstb, vstb, gs)
        xp = _post(xp, norm_w3, w_in_bf, yda, ysb, wda_bf, wsb_bf, wout_bf, fnw, layer, last)

        (qd, kd, vd, gd, qs, ks, vs, gs, kdb, vdb, ksb, vsb) = _inproj(
            xs, norm_w3, w_in_bf, layer, cos_s, sin_s)
        all_rows = N_PAIRS * 2 * dec_seq
        yda = _decode_attention(
            functools.partial(_da_decode_kernel, lam_init=lam_init, past_len=past_len), "da_decode",
            dec_seq, qd, cdk, cdv,
            pl.BlockSpec((None, None, CACHE_TILE * da_heads, PAIR), lambda b, j, layer=layer: (layer, b, j, 0)),
            kdb, vdb, gd,
            [pltpu.VMEM((all_rows, PAIR), BF16), pltpu.VMEM((all_rows, 1), F32),
             pltpu.VMEM((all_rows, 1), F32), pltpu.VMEM((all_rows, PAIR), F32)],
            lam_args, lam_specs)
        ysb = _sb_decode(layer, dec_seq, qs, cskt, csvt, ksb, vsb, gs)
        xs = _post(xs, norm_w3, w_in_bf, yda, ysb, wda_bf, wsb_bf, wout_bf, fnw, layer, last)
        rows_s.append((kd, vd, ks, vs))

    def stack(rows, idx, shape):
        return jnp.stack([r[idx] for r in rows], axis=0).reshape((depth,) + shape)

    def token_major(a):
        return jnp.transpose(a.reshape(depth, batch, sb_heads, HEAD_DIM, seq), (0, 1, 4, 2, 3))

    sb_s = (dec_batch, dec_seq, sb_heads, HEAD_DIM)
    da_s = (dec_batch, dec_seq, da_heads, PAIR)
    kd, vd, kst, vst = kv_stack
    return (xp.reshape(batch, seq, d), xs.reshape(dec_batch, dec_seq, d),
            kd.reshape(depth, batch, seq, da_heads, PAIR), vd.reshape(depth, batch, seq, da_heads, PAIR),
            token_major(kst), token_major(vst),
            stack(rows_s, 0, da_s), stack(rows_s, 1, da_s), stack(rows_s, 2, sb_s), stack(rows_s, 3, sb_s))
```
